```python
import math
import jax, jax.numpy as jnp
from jax import lax
import numpy as np

D_MODEL = 2048
BATCH = 4
SEQ = 2048
DEPTH = 1
DEC_BATCH = 2
DEC_SEQ = 4096
PAST_LEN = 128

MLA_HEADS = 8
Q_LORA = 512
KV_LORA = 256
QK_NOPE = 128
QK_ROPE = 64
QK_HEAD = QK_NOPE + QK_ROPE
V_HEAD = 128
MLA_WIDTH = MLA_HEADS * V_HEAD
ROPE_THETA = 10000.0
Q_BLOCK = 128
GDN_HEADS = 8
GDN_DK = 128
GDN_DV = 128
GDN_WIDTH = GDN_HEADS * GDN_DV
QKV_GDN = 2 * GDN_HEADS * GDN_DK + GDN_HEADS * GDN_DV
CONV_W = 5
CHUNK = 64
MIX_WIDTH = MLA_WIDTH + GDN_WIDTH
SPLITS = (Q_LORA, KV_LORA, QK_ROPE, QKV_GDN, GDN_WIDTH, GDN_HEADS, GDN_HEADS, GDN_HEADS, GDN_HEADS)
IN_COLS = Q_LORA + KV_LORA + QK_ROPE + QKV_GDN + GDN_WIDTH + 4 * GDN_HEADS
N_EXPERTS = 16
EXPERT_FF = 1024
CAPACITY_FACTOR = 2
EPS = 1e-6

kernel_name = "hybrid_mla_gdn_ec_encoder"


def rms_norm(x, g):
    xf = x.astype(jnp.float32)
    y = xf * lax.rsqrt(jnp.mean(xf * xf, axis=-1, keepdims=True) + EPS)
    return (y * g.astype(jnp.float32)).astype(x.dtype)


def l2_normalize(x):
    xf = x.astype(jnp.float32)
    return xf * lax.rsqrt(jnp.sum(xf * xf, axis=-1, keepdims=True) + EPS)


def split_cols(t, sizes):
    out = []
    off = 0
    for s in sizes:
        out.append(t[..., off:off + s])
        off += s
    return out


def apply_rope(x, pos):
    half = QK_ROPE // 2
    inv_freq = ROPE_THETA ** (-jnp.arange(half, dtype=jnp.float32) / half)
    ang = pos.astype(jnp.float32)[:, None] * inv_freq[None, :]
    cos = jnp.cos(ang)[None, :, None, :]
    sin = jnp.sin(ang)[None, :, None, :]
    x1 = x[..., :half].astype(jnp.float32)
    x2 = x[..., half:].astype(jnp.float32)
    return jnp.concatenate([x1 * cos - x2 * sin, x2 * cos + x1 * sin], axis=-1).astype(x.dtype)


def blocked_attention(q, k, v):
    B, H, S, Dq = q.shape
    nb = S // Q_BLOCK
    qb = q.reshape(B, H, nb, Q_BLOCK, Dq).transpose(2, 0, 1, 3, 4)
    scale = QK_HEAD ** -0.5

    def one_block(qi):
        s = jnp.einsum('bhqd,bhkd->bhqk', qi, k).astype(jnp.float32) * scale
        p = jax.nn.softmax(s, axis=-1).astype(v.dtype)
        return jnp.einsum('bhqk,bhkd->bhqd', p, v)

    o = lax.map(one_block, qb)
    return o.transpose(1, 0, 3, 2, 4).reshape(B, S, H * V_HEAD)


def mla_mixer(q_lat, kv_lat, k_rope, q_a_norm_g, w_q_b, kv_a_norm_g, w_kv_b, q_norm_g, k_norm_g):
    B, S, _ = q_lat.shape
    pos = jnp.arange(S)
    q = (rms_norm(q_lat, q_a_norm_g) @ w_q_b).reshape(B, S, MLA_HEADS, QK_HEAD)
    kv = (rms_norm(kv_lat, kv_a_norm_g) @ w_kv_b).reshape(B, S, MLA_HEADS, QK_NOPE + V_HEAD)
    k_nope, v = kv[..., :QK_NOPE], kv[..., QK_NOPE:]
    k = jnp.concatenate([k_nope, jnp.broadcast_to(k_rope[:, :, None, :], (B, S, MLA_HEADS, QK_ROPE))], axis=-1)
    q = rms_norm(q, q_norm_g)
    k = rms_norm(k, k_norm_g)
    q = jnp.concatenate([q[..., :QK_NOPE], apply_rope(q[..., QK_NOPE:], pos)], axis=-1)
    k = jnp.concatenate([k[..., :QK_NOPE], apply_rope(k[..., QK_NOPE:], pos)], axis=-1)
    return blocked_attention(q.transpose(0, 2, 1, 3), k.transpose(0, 2, 1, 3), v.transpose(0, 2, 1, 3))


def centred_depthwise_conv(x, w):
    C = x.shape[-1]
    pad = CONV_W // 2
    return lax.conv_general_dilated(x, w.reshape(CONV_W, 1, C).astype(x.dtype), window_strides=(1,),
                                    padding=[(pad, pad)], dimension_numbers=('NWC', 'WIO', 'NWC'),
                                    feature_group_count=C)


def chunk_gated_delta(q, k, v, g, beta):
    B, S, H, DK = q.shape
    DV = v.shape[-1]
    N = S // CHUNK

    def to_chunks(t):
        return jnp.swapaxes(t.reshape((B, N, CHUNK, H) + t.shape[3:]), 2, 3)

    q = to_chunks(q) * (DK ** -0.5)
    k = to_chunks(k)
    v = to_chunks(v)
    g = to_chunks(g)
    beta = to_chunks(beta)
    gc = jnp.cumsum(g, axis=-1)
    idx = jnp.arange(CHUNK)
    incl = idx[:, None] >= idx[None, :]
    strict = idx[:, None] > idx[None, :]
    decay = jnp.exp(jnp.where(incl, gc[..., :, None] - gc[..., None, :], -jnp.inf))
    kb = k * beta[..., None]
    L = jnp.where(strict, jnp.einsum('bnhid,bnhjd->bnhij', kb, k) * decay, 0.0)
    A = L + jnp.eye(CHUNK, dtype=L.dtype)
    rhs = jnp.concatenate([v * beta[..., None], kb * jnp.exp(gc)[..., None]], axis=-1)
    sol = lax.linalg.triangular_solve(A, rhs, left_side=True, lower=True, unit_diagonal=True)
    u, w = sol[..., :DV], sol[..., DV:]
    intra = jnp.where(incl, jnp.einsum('bnhid,bnhjd->bnhij', q, k) * decay, 0.0)
    q_dec = q * jnp.exp(gc)[..., None]
    g_last = gc[..., -1]
    k_dec = k * jnp.exp(g_last[..., None] - gc)[..., None]

    def step(state, xs):
        q_d, k_d, u_i, w_i, intra_i, gl = xs
        v_new = u_i - jnp.einsum('bhck,bhkv->bhcv', w_i, state)
        o = jnp.einsum('bhck,bhkv->bhcv', q_d, state) + jnp.einsum('bhcj,bhjv->bhcv', intra_i, v_new)
        state = state * jnp.exp(gl)[..., None, None] + jnp.einsum('bhck,bhcv->bhkv', k_d, v_new)
        return state, o

    xs = (jnp.swapaxes(q_dec, 0, 1), jnp.swapaxes(k_dec, 0, 1), jnp.swapaxes(u, 0, 1),
          jnp.swapaxes(w, 0, 1), jnp.swapaxes(intra, 0, 1), jnp.swapaxes(g_last, 0, 1))
    state0 = jnp.zeros((B, H, DK, DV), jnp.float32)
    _, o = lax.scan(step, state0, xs)
    return o.transpose(1, 0, 3, 2, 4).reshape(B, S, H, DV)


def gdn_mixer(qkv, z, a_f, a_b, b_f, b_b, conv_w, a_log_fwd, a_log_bwd, dt_bias_fwd, dt_bias_bwd, gdn_norm_g):
    B, S, _ = qkv.shape
    qkv = jax.nn.silu(centred_depthwise_conv(qkv, conv_w))
    q, k, v = split_cols(qkv, (GDN_HEADS * GDN_DK, GDN_HEADS * GDN_DK, GDN_HEADS * GDN_DV))
    q = l2_normalize(q.reshape(B, S, GDN_HEADS, GDN_DK))
    k = l2_normalize(k.reshape(B, S, GDN_HEADS, GDN_DK))
    v = v.reshape(B, S, GDN_HEADS, GDN_DV).astype(jnp.float32)
    g_f = -jnp.exp(a_log_fwd.astype(jnp.float32)) * jax.nn.softplus(a_f.astype(jnp.float32) + dt_bias_fwd.astype(jnp.float32))
    g_b = -jnp.exp(a_log_bwd.astype(jnp.float32)) * jax.nn.softplus(a_b.astype(jnp.float32) + dt_bias_bwd.astype(jnp.float32))
    beta_f = jax.nn.sigmoid(b_f.astype(jnp.float32))
    beta_b = jax.nn.sigmoid(b_b.astype(jnp.float32))
    o_f = chunk_gated_delta(q, k, v, g_f, beta_f)
    o_b = jnp.flip(chunk_gated_delta(jnp.flip(q, 1), jnp.flip(k, 1), jnp.flip(v, 1),
                                     jnp.flip(g_b, 1), jnp.flip(beta_b, 1)), 1)
    o = o_f + o_b
    o = o * lax.rsqrt(jnp.mean(o * o, axis=-1, keepdims=True) + EPS) * gdn_norm_g.astype(jnp.float32)
    o = o * jax.nn.silu(z.reshape(B, S, GDN_HEADS, GDN_DV).astype(jnp.float32))
    return o.reshape(B, S, GDN_WIDTH).astype(qkv.dtype)


def ec_moe(x, w_router, w_gate, w_up, w_down):
    B, S, D = x.shape
    T = B * S
    C = max(1, CAPACITY_FACTOR * T // N_EXPERTS)
    xt = x.reshape(T, D)
    aff = jax.nn.softmax((xt @ w_router).astype(jnp.float32), axis=-1)
    gates, idx = lax.top_k(aff.T, C)
    xe = xt[idx]
    h = jax.nn.silu(jnp.einsum('ecd,edf->ecf', xe, w_gate)) * jnp.einsum('ecd,edf->ecf', xe, w_up)
    ye = jnp.einsum('ecf,efd->ecd', h, w_down) * gates[..., None].astype(x.dtype)
    out = jnp.zeros_like(xt).at[idx.reshape(-1)].add(ye.reshape(-1, D))
    return out.reshape(B, S, D)


def encoder_layer(x, norm1_g, w_in, q_a_norm_g, w_q_b, kv_a_norm_g, w_kv_b, q_norm_g, k_norm_g,
                  conv_w, a_log_fwd, a_log_bwd, dt_bias_fwd, dt_bias_bwd, gdn_norm_g, w_o,
                  norm2_g, w_router, w_gate, w_up, w_down):
    h = rms_norm(x, norm1_g)
    proj = h @ w_in
    q_lat, kv_lat, k_rope, qkv, z, a_f, a_b, b_f, b_b = split_cols(proj, SPLITS)
    mla_out = mla_mixer(q_lat, kv_lat, k_rope, q_a_norm_g, w_q_b, kv_a_norm_g, w_kv_b, q_norm_g, k_norm_g)
    gdn_out = gdn_mixer(qkv, z, a_f, a_b, b_f, b_b, conv_w, a_log_fwd, a_log_bwd,
                        dt_bias_fwd, dt_bias_bwd, gdn_norm_g)
    x = x + jnp.concatenate([mla_out, gdn_out], axis=-1) @ w_o
    x = x + ec_moe(rms_norm(x, norm2_g), w_router, w_gate, w_up, w_down)
    return x


def setup_inputs(seed: int = 0) -> dict:
    key = jax.random.key(seed)
    ks = jax.random.split(key, 24)
    f32 = jnp.float32

    def nrm(k, shape, scale):
        return jax.random.normal(k, shape, f32) * scale

    def gain(k, n):
        return 1.0 + 0.02 * jax.random.normal(k, (DEPTH, n), f32)

    dt = jnp.exp(jax.random.uniform(ks[12], (2, DEPTH, GDN_HEADS), f32, math.log(1e-3), math.log(1e-1)))
    dt_bias = dt + jnp.log(-jnp.expm1(-dt))
    a_log = jnp.log(jax.random.uniform(ks[13], (2, DEPTH, GDN_HEADS), f32, 1.0, 16.0))
    return {
        "x_prompt": jax.random.normal(ks[0], (BATCH, SEQ, D_MODEL), f32),
        "x_sample": jax.random.normal(ks[1], (DEC_BATCH, DEC_SEQ, D_MODEL), f32),
        "norm1_g": gain(ks[2], D_MODEL),
        "w_in": nrm(ks[3], (DEPTH, D_MODEL, IN_COLS), D_MODEL ** -0.5),
        "q_a_norm_g": gain(ks[4], Q_LORA),
        "w_q_b": nrm(ks[5], (DEPTH, Q_LORA, MLA_HEADS * QK_HEAD), Q_LORA ** -0.5),
        "kv_a_norm_g": gain(ks[6], KV_LORA),
        "w_kv_b": nrm(ks[7], (DEPTH, KV_LORA, MLA_HEADS * (QK_NOPE + V_HEAD)), KV_LORA ** -0.5),
        "q_norm_g": gain(ks[8], QK_HEAD),
        "k_norm_g": gain(ks[9], QK_HEAD),
        "conv_w": nrm(ks[10], (DEPTH, CONV_W, QKV_GDN), CONV_W ** -0.5),
        "a_log_fwd": a_log[0],
        "a_log_bwd": a_log[1],
        "dt_bias_fwd": dt_bias[0],
        "dt_bias_bwd": dt_bias[1],
        "gdn_norm_g": gain(ks[11], GDN_DV),
        "w_o": nrm(ks[14], (DEPTH, MIX_WIDTH, D_MODEL), MIX_WIDTH ** -0.5),
        "norm2_g": gain(ks[15], D_MODEL),
        "w_router": nrm(ks[16], (DEPTH, D_MODEL, N_EXPERTS), D_MODEL ** -0.5),
        "w_gate": nrm(ks[17], (DEPTH, N_EXPERTS, D_MODEL, EXPERT_FF), D_MODEL ** -0.5),
        "w_up": nrm(ks[18], (DEPTH, N_EXPERTS, D_MODEL, EXPERT_FF), D_MODEL ** -0.5),
        "w_down": nrm(ks[19], (DEPTH, N_EXPERTS, EXPERT_FF, D_MODEL), EXPERT_FF ** -0.5),
    }


def reference(x_prompt, x_sample, norm1_g, w_in, q_a_norm_g, w_q_b, kv_a_norm_g, w_kv_b, q_norm_g,
              k_norm_g, conv_w, a_log_fwd, a_log_bwd, dt_bias_fwd, dt_bias_bwd, gdn_norm_g, w_o,
              norm2_g, w_router, w_gate, w_up, w_down):
    y_prompt = x_prompt
    y_sample = x_sample
    for l in range(DEPTH):
        p = (norm1_g[l], w_in[l], q_a_norm_g[l], w_q_b[l], kv_a_norm_g[l], w_kv_b[l], q_norm_g[l],
             k_norm_g[l], conv_w[l], a_log_fwd[l], a_log_bwd[l], dt_bias_fwd[l], dt_bias_bwd[l],
             gdn_norm_g[l], w_o[l], norm2_g[l], w_router[l], w_gate[l], w_up[l], w_down[l])
        y_prompt = encoder_layer(y_prompt, *p)
        y_sample = encoder_layer(y_sample, *p)
    return (y_prompt, y_sample)
```

```python
import functools
import math

import jax
import jax.numpy as jnp
from jax import lax
from jax.experimental import pallas as pl
from jax.experimental.pallas import tpu as pltpu

F32 = jnp.float32
BF16 = jnp.bfloat16
I32 = jnp.int32

D_MODEL = 2048
MLA_HEADS = 8
Q_LORA = 512
KV_LORA = 256
QK_NOPE = 128
QK_ROPE = 64
QK_HEAD = QK_NOPE + QK_ROPE
QK_PAD = 256
V_HEAD = 128
ROPE_THETA = 10000.0
GDN_HEADS = 8
GDN_DK = 128
GDN_DV = 128
QKV_GDN = 2 * GDN_HEADS * GDN_DK + GDN_HEADS * GDN_DV
GDN_WIDTH = GDN_HEADS * GDN_DV
MLA_WIDTH = MLA_HEADS * V_HEAD
CONV_W = 5
CHUNK = 64
N_EXPERTS = 16
EXPERT_FF = 1024
CAPACITY_FACTOR = 2
EPS = 1e-6

LANE = 128
BF16_ROWS = 16
MIB = 1024 * 1024

NA = Q_LORA + KV_LORA + GDN_WIDTH
NB = QKV_GDN + LANE
Z_COL_BLOCK = (Q_LORA + KV_LORA) // LANE
MISC_COL_BLOCK = QKV_GDN // LANE

NN = (((1,), (0,)), ((), ()))
NT = (((1,), (1,)), ((), ()))
TN = (((0,), (0,)), ((), ()))


def _params(semantics, vmem_mib):
    return pltpu.CompilerParams(dimension_semantics=semantics, vmem_limit_bytes=vmem_mib * MIB)


def _resident(shape):
    nd = len(shape)
    return pl.BlockSpec(shape, lambda *_: (0,) * nd, pipeline_mode=pl.Buffered(1))


def _dot(a, b, dims=NN):
    return lax.dot_general(a, b, dims, preferred_element_type=F32)


def _dotb(a, b, dims=NN):
    return lax.dot_general(a.astype(BF16), b.astype(BF16), dims, preferred_element_type=F32)


def _split(x):
    hi = x.astype(BF16)
    lo = (x - hi.astype(F32)).astype(BF16)
    return hi, lo


def _dot3(a, b, dims=NN):
    ah, al = _split(a)
    bh, bl = _split(b)
    return _dot(ah, bh, dims) + (_dot(ah, bl, dims) + _dot(al, bh, dims))


def _silu(x):
    return x * jax.nn.sigmoid(x)


def _in_proj_kernel(x_ref, g_ref, wa_ref, wb_ref, oa_ref, ob_ref):
    x = x_ref[...]
    ms = jnp.mean(x * x, axis=-1, keepdims=True)
    h = (x * lax.rsqrt(ms + EPS) * g_ref[...]).astype(BF16)
    oa_ref[...] = _dot(h, wa_ref[...]).astype(BF16)
    ob_ref[...] = _dot(h, wb_ref[...])


def _in_proj(x2d, g, wa, wb):
    t = x2d.shape[0]
    tm = min(512, t)
    return pl.pallas_call(
        _in_proj_kernel,
        grid=(t // tm,),
        in_specs=[
            pl.BlockSpec((tm, D_MODEL), lambda i: (i, 0)),
            _resident((1, D_MODEL)),
            _resident(wa.shape),
            _resident(wb.shape),
        ],
        out_specs=[
            pl.BlockSpec((tm, NA), lambda i: (i, 0)),
            pl.BlockSpec((tm, NB), lambda i: (i, 0)),
        ],
        out_shape=[
            jax.ShapeDtypeStruct((t, NA), BF16),
            jax.ShapeDtypeStruct((t, NB), F32),
        ],
        compiler_params=_params(("parallel",), 56),
        name="in_proj",
    )(x2d, g, wa, wb)


def _mla_prep_kernel(ql_ref, kvl_ref, misc_ref, cos_ref, sin_ref, gqa_ref, gkva_ref, wq_ref, wkv_ref,
                     gq_ref, gk_ref, q_ref, k_ref, v_ref):
    def norm(x, g):
        return x * lax.rsqrt(jnp.mean(x * x, axis=-1, keepdims=True) + EPS) * g

    qn = norm(ql_ref[...].astype(F32), gqa_ref[...]).astype(BF16)
    kvn = norm(kvl_ref[...].astype(F32), gkva_ref[...]).astype(BF16)
    q = _dot(qn, wq_ref[...])
    kv = _dot(kvn, wkv_ref[...])

    cos = cos_ref[...]
    sin = sin_ref[...]
    lane = lax.broadcasted_iota(I32, cos.shape, 1)

    def rope(t):
        rot = jnp.where(lane < QK_ROPE // 2, pltpu.roll(t, LANE - QK_ROPE // 2, 1), pltpu.roll(t, QK_ROPE // 2, 1))
        return t * cos + rot * sin

    gq = gq_ref[...]
    gk = gk_ref[...]
    kr = jnp.where(lane < QK_ROPE, misc_ref[...], 0.0)
    kr_ssq = jnp.sum(kr * kr, axis=-1, keepdims=True)
    kr_rot = rope(kr * gk[:, QK_NOPE:])
    q_scale = QK_HEAD ** -0.5
    for h in range(MLA_HEADS):
        qh = q[:, h * QK_PAD:(h + 1) * QK_PAD]
        sc = lax.rsqrt(jnp.sum(qh * qh, axis=-1, keepdims=True) * (1.0 / QK_HEAD) + EPS) * q_scale
        q_ref[0, h, :, :QK_NOPE] = (qh[:, :QK_NOPE] * gq[:, :QK_NOPE] * sc).astype(BF16)
        q_ref[0, h, :, QK_NOPE:] = (rope(qh[:, QK_NOPE:] * gq[:, QK_NOPE:]) * sc).astype(BF16)
        kn = kv[:, h * 256:h * 256 + QK_NOPE]
        sk = lax.rsqrt((jnp.sum(kn * kn, axis=-1, keepdims=True) + kr_ssq) * (1.0 / QK_HEAD) + EPS)
        k_ref[0, h, :, :QK_NOPE] = (kn * gk[:, :QK_NOPE] * sk).astype(BF16)
        k_ref[0, h, :, QK_NOPE:] = (kr_rot * sk).astype(BF16)
        v_ref[0, h] = kv[:, h * 256 + QK_NOPE:(h + 1) * 256].astype(BF16)


def _mla_prep(oa, ob, cos, sin, gqa, gkva, wq, wkv, gq, gk, bt, s):
    tm = min(512, s)
    nt = s // tm
    row = lambda b, i: (b * nt + i, 0)
    return pl.pallas_call(
        _mla_prep_kernel,
        grid=(bt, nt),
        in_specs=[
            pl.BlockSpec((tm, Q_LORA), row),
            pl.BlockSpec((tm, KV_LORA), lambda b, i: (b * nt + i, Q_LORA // KV_LORA)),
            pl.BlockSpec((tm, LANE), lambda b, i: (b * nt + i, MISC_COL_BLOCK)),
            pl.BlockSpec((tm, LANE), lambda b, i: (i, 0)),
            pl.BlockSpec((tm, LANE), lambda b, i: (i, 0)),
            _resident(gqa.shape),
            _resident(gkva.shape),
            _resident(wq.shape),
            _resident(wkv.shape),
            _resident(gq.shape),
            _resident(gk.shape),
        ],
        out_specs=[
            pl.BlockSpec((1, MLA_HEADS, tm, QK_PAD), lambda b, i: (b, 0, i, 0)),
            pl.BlockSpec((1, MLA_HEADS, tm, QK_PAD), lambda b, i: (b, 0, i, 0)),
            pl.BlockSpec((1, MLA_HEADS, tm, V_HEAD), lambda b, i: (b, 0, i, 0)),
        ],
        out_shape=[
            jax.ShapeDtypeStruct((bt, MLA_HEADS, s, QK_PAD), BF16),
            jax.ShapeDtypeStruct((bt, MLA_HEADS, s, QK_PAD), BF16),
            jax.ShapeDtypeStruct((bt, MLA_HEADS, s, V_HEAD), BF16),
        ],
        compiler_params=_params(("parallel", "parallel"), 48),
        name="mla_prep",
    )(oa, oa, ob, cos, sin, gqa, gkva, wq, wkv, gq, gk)


def _attn_kernel(q_ref, k_ref, v_ref, o_ref):
    s = _dot(q_ref[0, 0], k_ref[0, 0], NT)
    m = jnp.max(s, axis=-1, keepdims=True)
    p = jnp.exp(s - m)
    l = jnp.sum(p, axis=-1, keepdims=True)
    o = _dot(p.astype(BF16), v_ref[0, 0])
    o_ref[0] = (o / l).astype(BF16)


def _attention(q, k, v):
    bt, h, s, _ = q.shape
    tq = min(256, s)
    return pl.pallas_call(
        _attn_kernel,
        grid=(bt, h, s // tq),
        in_specs=[
            pl.BlockSpec((1, 1, tq, QK_PAD), lambda b, hh, i: (b, hh, i, 0)),
            pl.BlockSpec((1, 1, s, QK_PAD), lambda b, hh, i: (b, hh, 0, 0)),
            pl.BlockSpec((1, 1, s, V_HEAD), lambda b, hh, i: (b, hh, 0, 0)),
        ],
        out_specs=pl.BlockSpec((1, tq, V_HEAD), lambda b, hh, i: (b, i, hh)),
        out_shape=jax.ShapeDtypeStruct((bt, s, MLA_WIDTH), BF16),
        compiler_params=_params(("parallel", "parallel", "parallel"), 48),
        name="attention",
    )(q, k, v)


def _conv_kernel(x_ref, w_ref, o_ref):
    c = pl.program_id(1)
    x = x_ref[0]
    s = x.shape[0]
    w = w_ref[...]
    row = lax.broadcasted_iota(I32, x.shape, 0)
    pad = CONV_W // 2
    acc = x * w[pad:pad + 1]
    for d in range(-pad, pad + 1):
        if d == 0:
            continue
        xs = pltpu.roll(x, (-d) % s, 0)
        ok = jnp.logical_and(row + d >= 0, row + d < s)
        acc = acc + jnp.where(ok, xs, 0.0) * w[pad + d:pad + d + 1]
    y = _silu(acc)
    inv = lax.rsqrt(jnp.sum(y * y, axis=-1, keepdims=True) + EPS)
    is_q = c < GDN_HEADS
    is_qk = c < 2 * GDN_HEADS
    scale = jnp.where(is_qk, inv, 1.0) * jnp.where(is_q, GDN_DK ** -0.5, 1.0)
    o_ref[0, 0] = y * scale


def _conv(ob3, conv_w):
    bt, s, _ = ob3.shape
    nblk = QKV_GDN // LANE
    return pl.pallas_call(
        _conv_kernel,
        grid=(bt, nblk),
        in_specs=[
            pl.BlockSpec((1, s, LANE), lambda b, c: (b, 0, c)),
            pl.BlockSpec((CONV_W, LANE), lambda b, c: (0, c)),
        ],
        out_specs=pl.BlockSpec((1, 1, s, LANE), lambda b, c: (b, c, 0, 0)),
        out_shape=jax.ShapeDtypeStruct((bt, nblk, s, LANE), F32),
        compiler_params=_params(("parallel", "parallel"), 48),
        name="gdn_conv",
    )(ob3, conv_w)


def _gates_kernel(x_ref, alog_ref, dtb_ref, o_ref):
    x = x_ref[0]
    nh = GDN_HEADS
    a = x[:2 * nh] + dtb_ref[...]
    softplus = jnp.maximum(a, 0.0) + jnp.log1p(jnp.exp(-jnp.abs(a)))
    g = -jnp.exp(alog_ref[...]) * softplus
    beta = jax.nn.sigmoid(x[2 * nh:])
    gf = g[:nh]
    gb = g[nh:]
    pos = lax.broadcasted_iota(I32, gf.shape, 2) % CHUNK
    sh = 1
    while sh < CHUNK:
        gf = gf + jnp.where(pos >= sh, pltpu.roll(gf, sh, 2), 0.0)
        gb = gb + jnp.where(pos < CHUNK - sh, pltpu.roll(gb, LANE - sh, 2), 0.0)
        sh *= 2
    o_ref[0, :nh] = gf
    o_ref[0, nh:2 * nh] = gb
    o_ref[0, 2 * nh:] = beta


def _gates(gt, alog, dtb):
    bt, r, sl, _ = gt.shape
    return pl.pallas_call(
        _gates_kernel,
        grid=(bt,),
        in_specs=[
            pl.BlockSpec((1, r, sl, LANE), lambda b: (b, 0, 0, 0)),
            _resident(alog.shape),
            _resident(dtb.shape),
        ],
        out_specs=pl.BlockSpec((1, r, sl, LANE), lambda b: (b, 0, 0, 0)),
        out_shape=jax.ShapeDtypeStruct(gt.shape, F32),
        compiler_params=_params(("parallel",), 32),
        name="gdn_gates",
    )(gt, alog, dtb)


def _unit_tri_inverse(lmat, eye):
    p = eye - lmat
    m = _dotb(lmat, lmat)
    n = 2
    while n < CHUNK // 2:
        p = p + _dotb(p, m)
        m = _dotb(m, m)
        n *= 2
    p = p + _dotb(p, m)
    resid = eye - (p + _dot3(lmat, p))
    return p + _dotb(p, resid)


def _gdn_kernel(q_ref, k_ref, v_ref, grf_ref, grb_ref, gc_ref, z_ref, gn_ref, o_ref,
                uf_ref, wf_ref, ub_ref, wb_ref, inf_ref, inb_ref, oacc_ref):
    s = q_ref.shape[2]
    n_chunks = s // CHUNK
    ri = lax.broadcasted_iota(I32, (CHUNK, CHUNK), 0)
    ci = lax.broadcasted_iota(I32, (CHUNK, CHUNK), 1)
    eye = (ri == ci).astype(F32)
    lower_incl = ri >= ci
    lower_strict = ri > ci
    upper_incl = ri <= ci
    upper_strict = ri < ci

    def chunk_prepare(n, carry):
        r0 = pl.multiple_of(n * CHUNK, CHUNK)
        rows = pl.ds(r0, CHUNK)
        q = q_ref[0, 0, rows, :]
        k = k_ref[0, 0, rows, :]
        v = v_ref[0, 0, rows, :]
        g4 = gc_ref[0, 0, rows, :]
        kk = _dot3(k, k, NT)
        qk = _dot3(q, k, NT)
        dirs = (
            (g4[:, 0:1], grf_ref[0, 0, pl.ds(n, 1), :], g4[:, 2:3], lower_incl, lower_strict, uf_ref, wf_ref, inf_ref),
            (g4[:, 1:2], grb_ref[0, 0, pl.ds(n, 1), :], g4[:, 3:4], upper_incl, upper_strict, ub_ref, wb_ref, inb_ref),
        )
        for gcol, grow, beta, incl, strict, u_ref, w_ref, in_ref in dirs:
            decay = jnp.exp(jnp.where(incl, gcol - grow, -jnp.inf))
            lmat = jnp.where(strict, beta * kk * decay, 0.0)
            tinv = _unit_tri_inverse(lmat, eye)
            rhs = jnp.concatenate([v * beta, k * (beta * jnp.exp(gcol))], axis=-1)
            sol = _dot3(tinv, rhs)
            u_ref[rows, :] = sol[:, :GDN_DV]
            w_ref[rows, :] = sol[:, GDN_DV:]
            in_ref[rows, :] = jnp.where(incl, qk * decay, 0.0)
        return carry

    lax.fori_loop(0, n_chunks, chunk_prepare, 0)
    oacc_ref[...] = jnp.zeros_like(oacc_ref)

    def chunk_step(rows, col, last_row, state, u_ref, w_ref, in_ref):
        q = q_ref[0, 0, rows, :]
        k = k_ref[0, 0, rows, :]
        g4 = gc_ref[0, 0, rows, :]
        gc = g4[:, col:col + 1]
        g_last = g4[last_row:last_row + 1, col:col + 1]
        v_new = u_ref[rows, :] - _dotb(w_ref[rows, :], state)
        o = _dotb(q * jnp.exp(gc), state) + _dotb(in_ref[rows, :], v_new)
        oacc_ref[rows, :] += o
        k_dec = k * jnp.exp(g_last - gc)
        return state * jnp.exp(g_last) + _dotb(k_dec, v_new, TN)

    def scan_step(n, carry):
        sf, sb = carry
        rf = pl.ds(pl.multiple_of(n * CHUNK, CHUNK), CHUNK)
        rb = pl.ds(pl.multiple_of((n_chunks - 1 - n) * CHUNK, CHUNK), CHUNK)
        sf = chunk_step(rf, 0, CHUNK - 1, sf, uf_ref, wf_ref, inf_ref)
        sb = chunk_step(rb, 1, 0, sb, ub_ref, wb_ref, inb_ref)
        return sf, sb

    zero = jnp.zeros((GDN_DK, GDN_DV), F32)
    lax.fori_loop(0, n_chunks, scan_step, (zero, zero))

    o = oacc_ref[...]
    on = o * lax.rsqrt(jnp.mean(o * o, axis=-1, keepdims=True) + EPS) * gn_ref[...]
    o_ref[0] = (on * _silu(z_ref[0].astype(F32))).astype(BF16)


def _gdn(qkv_h, g_rows, g_cols, oa3, gn):
    bt, _, s, _ = qkv_h.shape
    nh = GDN_HEADS
    n_chunks = s // CHUNK
    head = lambda off: (lambda b, h: (b, off + h, 0, 0))
    seq = pltpu.VMEM((s, GDN_DV), F32)
    tri = pltpu.VMEM((s, CHUNK), F32)
    return pl.pallas_call(
        _gdn_kernel,
        grid=(bt, nh),
        in_specs=[
            pl.BlockSpec((1, 1, s, GDN_DK), head(0)),
            pl.BlockSpec((1, 1, s, GDN_DK), head(nh)),
            pl.BlockSpec((1, 1, s, GDN_DV), head(2 * nh)),
            pl.BlockSpec((1, 1, n_chunks, CHUNK), head(0)),
            pl.BlockSpec((1, 1, n_chunks, CHUNK), head(nh)),
            pl.BlockSpec((1, 1, s, 4), head(0)),
            pl.BlockSpec((1, s, GDN_DV), lambda b, h: (b, 0, Z_COL_BLOCK + h)),
            _resident(gn.shape),
        ],
        out_specs=pl.BlockSpec((1, s, GDN_DV), lambda b, h: (b, 0, h)),
        out_shape=jax.ShapeDtypeStruct((bt, s, GDN_WIDTH), BF16),
        scratch_shapes=[seq, seq, seq, seq, tri, tri, seq],
        compiler_params=_params(("parallel", "parallel"), 56),
        name="gdn",
    )(qkv_h, qkv_h, qkv_h, g_rows, g_rows, g_cols, oa3, gn)


def _out_proj_kernel(x_ref, mla_ref, gdn_ref, wo_ref, g_ref, wr_ref, x1_ref, xn_ref, aff_ref):
    y = x_ref[...] + _dot(mla_ref[...], wo_ref[:MLA_WIDTH, :]) + _dot(gdn_ref[...], wo_ref[MLA_WIDTH:, :])
    x1_ref[...] = y
    xn = y * lax.rsqrt(jnp.mean(y * y, axis=-1, keepdims=True) + EPS) * g_ref[...]
    xn_ref[...] = xn
    logits = _dot3(xn, wr_ref[...])
    e = jnp.exp(logits - jnp.max(logits, axis=-1, keepdims=True))
    aff_ref[...] = e / jnp.sum(e, axis=-1, keepdims=True)


def _out_proj(x2d, mla, gdn, wo, g2, wr):
    t = x2d.shape[0]
    tm = min(256, t)
    rows = lambda w: pl.BlockSpec((tm, w), lambda i: (i, 0))
    return pl.pallas_call(
        _out_proj_kernel,
        grid=(t // tm,),
        in_specs=[rows(D_MODEL), rows(MLA_WIDTH), rows(GDN_WIDTH), _resident(wo.shape), _resident(g2.shape),
                  _resident(wr.shape)],
        out_specs=[rows(D_MODEL), rows(D_MODEL), rows(N_EXPERTS)],
        out_shape=[
            jax.ShapeDtypeStruct((t, D_MODEL), F32),
            jax.ShapeDtypeStruct((t, D_MODEL), F32),
            jax.ShapeDtypeStruct((t, N_EXPERTS), F32),
        ],
        compiler_params=_params(("parallel",), 48),
        name="out_proj",
    )(x2d, mla, gdn, wo, g2, wr)


def _ffn_kernel(idx_ref, xn_hbm, gate_ref, wg_ref, wu_ref, wd_ref, ye_ref, xe32_ref, xe_ref, acc_ref, sem):
    f = pl.program_id(1)
    cap = xe_ref.shape[0]

    def row_copy(slot):
        tok = idx_ref[0, 0, slot]
        return pltpu.make_async_copy(xn_hbm.at[pl.ds(tok, 1), :], xe32_ref.at[pl.ds(slot, 1), :], sem.at[0])

    @pl.when(f == 0)
    def _():
        def start(slot, c):
            row_copy(slot).start()
            return c

        def wait(slot, c):
            row_copy(slot).wait()
            return c

        lax.fori_loop(0, cap, start, 0)
        lax.fori_loop(0, cap, wait, 0)
        xe_ref[...] = xe32_ref[...].astype(BF16)
        acc_ref[...] = jnp.zeros_like(acc_ref)

    xe = xe_ref[...]
    g = _dot(xe, wg_ref[0].astype(BF16))
    u = _dot(xe, wu_ref[0].astype(BF16))
    h = (_silu(g) * u).astype(BF16)
    acc_ref[...] += _dot(h, wd_ref[0].astype(BF16))

    @pl.when(f == pl.num_programs(1) - 1)
    def _():
        ye_ref[0] = (acc_ref[...] * gate_ref[0]).astype(BF16)


def _ffn(idx3, xn, gates3, w_gate, w_up, w_down):
    e, _, cap = idx3.shape
    tf = 256
    return pl.pallas_call(
        _ffn_kernel,
        grid=(e, EXPERT_FF // tf),
        in_specs=[
            pl.BlockSpec((1, 1, cap), lambda ee, f: (ee, 0, 0), memory_space=pltpu.SMEM),
            pl.BlockSpec(memory_space=pl.ANY),
            pl.BlockSpec((1, cap, 1), lambda ee, f: (ee, 0, 0)),
            pl.BlockSpec((1, D_MODEL, tf), lambda ee, f: (ee, 0, f)),
            pl.BlockSpec((1, D_MODEL, tf), lambda ee, f: (ee, 0, f)),
            pl.BlockSpec((1, tf, D_MODEL), lambda ee, f: (ee, f, 0)),
        ],
        out_specs=pl.BlockSpec((1, cap, D_MODEL), lambda ee, f: (ee, 0, 0)),
        out_shape=jax.ShapeDtypeStruct((e, cap, D_MODEL), BF16),
        scratch_shapes=[
            pltpu.VMEM((cap, D_MODEL), F32),
            pltpu.VMEM((cap, D_MODEL), BF16),
            pltpu.VMEM((cap, D_MODEL), F32),
            pltpu.SemaphoreType.DMA((1,)),
        ],
        compiler_params=_params(("arbitrary", "arbitrary"), 56),
        name="moe_ffn",
    )(idx3, xn, gates3, w_gate, w_up, w_down)


def _combine_kernel(lo_ref, slot_ref, x1_ref, ye_hbm, y_ref, win_ref, sem):
    i = pl.program_id(0)
    nt = pl.num_programs(0)
    n_e = win_ref.shape[1]
    width = win_ref.shape[2]

    def window_copy(tile, buf, e):
        lo = pl.multiple_of(lo_ref[tile * n_e + e], BF16_ROWS)
        return pltpu.make_async_copy(ye_hbm.at[e, pl.ds(lo, width), :], win_ref.at[buf, e], sem.at[buf, e])

    @pl.when(i == 0)
    def _():
        for e in range(n_e):
            window_copy(0, 0, e).start()

    @pl.when(i + 1 < nt)
    def _():
        for e in range(n_e):
            window_copy(i + 1, (i + 1) % 2, e).start()

    buf = i % 2
    slot = slot_ref[...]
    acc = x1_ref[...]
    col = lax.broadcasted_iota(I32, (slot.shape[0], width), 1)
    for e in range(n_e):
        window_copy(i, buf, e).wait()
        rel = slot[:, e:e + 1] - lo_ref[i * n_e + e]
        onehot = (rel == col).astype(BF16)
        acc = acc + _dot(onehot, win_ref[buf, e])
    y_ref[...] = acc


def _combine(lo_flat, slot, x1, ye, tt):
    t = x1.shape[0]
    n_e = ye.shape[0]
    width = tt + BF16_ROWS
    grid_spec = pltpu.PrefetchScalarGridSpec(
        num_scalar_prefetch=1,
        grid=(t // tt,),
        in_specs=[
            pl.BlockSpec((tt, n_e), lambda i, lo: (i, 0)),
            pl.BlockSpec((tt, D_MODEL), lambda i, lo: (i, 0)),
            pl.BlockSpec(memory_space=pl.ANY),
        ],
        out_specs=pl.BlockSpec((tt, D_MODEL), lambda i, lo: (i, 0)),
        scratch_shapes=[
            pltpu.VMEM((2, n_e, width, D_MODEL), BF16),
            pltpu.SemaphoreType.DMA((2, n_e)),
        ],
    )
    return pl.pallas_call(
        _combine_kernel,
        grid_spec=grid_spec,
        out_shape=jax.ShapeDtypeStruct((t, D_MODEL), F32),
        compiler_params=_params(("arbitrary",), 48),
        name="moe_combine",
    )(lo_flat, slot, x1, ye)


def _rope_tables(s):
    half = QK_ROPE // 2
    inv_freq = ROPE_THETA ** (-jnp.arange(half, dtype=F32) / half)
    ang = jnp.arange(s, dtype=F32)[:, None] * inv_freq[None, :]
    cos = jnp.cos(ang)
    sin = jnp.sin(ang)
    zeros = jnp.zeros((s, LANE - QK_ROPE), F32)
    return jnp.concatenate([cos, cos, zeros], axis=1), jnp.concatenate([-sin, sin, zeros], axis=1)


def _prepare_weights(norm1_g, w_in, q_a_norm_g, w_q_b, kv_a_norm_g, w_kv_b, q_norm_g, k_norm_g, conv_w,
                     a_log_fwd, a_log_bwd, dt_bias_fwd, dt_bias_bwd, gdn_norm_g, w_o, norm2_g, w_router):
    o_kv = Q_LORA
    o_kr = o_kv + KV_LORA
    o_qkv = o_kr + QK_ROPE
    o_z = o_qkv + QKV_GDN
    o_g = o_z + GDN_WIDTH
    wa = jnp.concatenate([w_in[:, :o_kr], w_in[:, o_z:o_g]], axis=1).astype(BF16)
    wb = jnp.concatenate(
        [w_in[:, o_qkv:o_z], w_in[:, o_kr:o_qkv], w_in[:, o_g:], jnp.zeros((D_MODEL, LANE - QK_ROPE - 4 * GDN_HEADS), F32)],
        axis=1).astype(BF16)
    wq = jnp.pad(w_q_b.reshape(Q_LORA, MLA_HEADS, QK_HEAD), ((0, 0), (0, 0), (0, QK_PAD - QK_HEAD)))
    wq = wq.reshape(Q_LORA, MLA_HEADS * QK_PAD).astype(BF16)
    pad_g = lambda g: jnp.pad(g, (0, QK_PAD - QK_HEAD)).reshape(1, QK_PAD)
    alog = jnp.concatenate([a_log_fwd, a_log_bwd]).reshape(2 * GDN_HEADS, 1, 1)
    dtb = jnp.concatenate([dt_bias_fwd, dt_bias_bwd]).reshape(2 * GDN_HEADS, 1, 1)
    return dict(
        g1=norm1_g.reshape(1, D_MODEL), wa=wa, wb=wb,
        gqa=q_a_norm_g.reshape(1, Q_LORA), gkva=kv_a_norm_g.reshape(1, KV_LORA),
        wq=wq, wkv=w_kv_b.astype(BF16), gq=pad_g(q_norm_g), gk=pad_g(k_norm_g),
        conv_w=conv_w,
        alog=jnp.broadcast_to(alog, (2 * GDN_HEADS, 1, LANE)), dtb=jnp.broadcast_to(dtb, (2 * GDN_HEADS, 1, LANE)),
        gn=gdn_norm_g.reshape(1, GDN_DV), wo=w_o.astype(BF16), g2=norm2_g.reshape(1, D_MODEL), wr=w_router,
    )


def _route(aff, cap, tt):
    t, n_e = aff.shape
    gates, idx = lax.top_k(aff.T, cap)
    order = jnp.argsort(idx, axis=1)
    idx = jnp.take_along_axis(idx, order, axis=1)
    gates = jnp.take_along_axis(gates, order, axis=1)
    e_ids = jnp.broadcast_to(jnp.arange(n_e, dtype=I32)[:, None], idx.shape)
    s_ids = jnp.broadcast_to(jnp.arange(cap, dtype=I32)[None, :], idx.shape)
    slot = jnp.full((t, n_e), -1, I32).at[idx, e_ids].set(s_ids)
    per_tile = (slot >= 0).reshape(t // tt, tt, n_e).sum(axis=1).astype(I32)
    lo = jnp.cumsum(per_tile, axis=0) - per_tile
    lo = jnp.minimum((lo // BF16_ROWS) * BF16_ROWS, cap - (tt + BF16_ROWS))
    return idx.astype(I32), gates, slot, lo.reshape(-1)


def _encoder_layer(x, p, w_gate, w_up, w_down):
    bt, s, _ = x.shape
    t = bt * s
    x2d = x.reshape(t, D_MODEL)
    oa, ob = _in_proj(x2d, p["g1"], p["wa"], p["wb"])

    cos, sin = _rope_tables(s)
    q, k, v = _mla_prep(oa, ob, cos, sin, p["gqa"], p["gkva"], p["wq"], p["wkv"], p["gq"], p["gk"], bt, s)
    mla_out = _attention(q, k, v)

    ob3 = ob.reshape(bt, s, NB)
    qkv_h = _conv(ob3, p["conv_w"])
    g_raw = ob3[:, :, QKV_GDN + QK_ROPE:QKV_GDN + QK_ROPE + 4 * GDN_HEADS]
    g_t = jnp.swapaxes(g_raw, 1, 2).reshape(bt, 4 * GDN_HEADS, s // LANE, LANE)
    g_out = _gates(g_t, p["alog"], p["dtb"]).reshape(bt, 4, GDN_HEADS, s)
    g_rows = g_out[:, :2].reshape(bt, 2 * GDN_HEADS, s // CHUNK, CHUNK)
    g_cols = jnp.transpose(g_out, (0, 2, 3, 1))
    gdn_out = _gdn(qkv_h, g_rows, g_cols, oa.reshape(bt, s, NA), p["gn"])

    x1, xn, aff = _out_proj(x2d, mla_out.reshape(t, MLA_WIDTH), gdn_out.reshape(t, GDN_WIDTH), p["wo"], p["g2"],
                            p["wr"])

    cap = max(1, CAPACITY_FACTOR * t // N_EXPERTS)
    tt = 128
    idx, gates, slot, lo = _route(aff, cap, tt)
    ye = _ffn(idx.reshape(N_EXPERTS, 1, cap), xn, gates.reshape(N_EXPERTS, cap, 1), w_gate, w_up, w_down)
    y = _combine(lo, slot, x1, ye, tt)
    return y.reshape(bt, s, D_MODEL)


def kernel(x_prompt, x_sample, norm1_g, w_in, q_a_norm_g, w_q_b, kv_a_norm_g, w_kv_b, q_norm_g, k_norm_g, conv_w,
           a_log_fwd, a_log_bwd, dt_bias_fwd, dt_bias_bwd, gdn_norm_g, w_o, norm2_g, w_router, w_gate, w_up, w_down):
    y_prompt = x_prompt
    y_sample = x_sample
    for l in range(norm1_g.shape[0]):
        p = _prepare_weights(norm1_g[l], w_in[l], q_a_norm_g[l], w_q_b[l], kv_a_norm_g[l], w_kv_b[l], q_norm_g[l],
                             k_norm_g[l], conv_w[l], a_log_fwd[l], a_log_bwd[l], dt_bias_fwd[l], dt_bias_bwd[l],
                             gdn_norm_g[l], w_o[l], norm2_g[l], w_router[l])
        y_prompt = _encoder_layer(y_prompt, p, w_gate[l], w_up[l], w_down[l])
        y_sample = _encoder_layer(y_sample, p, w_gate[l], w_up[l], w_down[l])
    return (y_prompt, y_sample)
```

```python
import functools
import math

import jax
import jax.numpy as jnp
from jax import lax
from jax.experimental import pallas as pl
from jax.experimental.pallas import tpu as pltpu

F32 = jnp.float32
BF16 = jnp.bfloat16
I32 = jnp.int32

D_MODEL = 2048
MLA_HEADS = 8
Q_LORA = 512
KV_LORA = 256
QK_NOPE = 128
QK_ROPE = 64
QK_HEAD = QK_NOPE + QK_ROPE
QK_PAD = 256
V_HEAD = 128
ROPE_THETA = 10000.0
GDN_HEADS = 8
GDN_DK = 128
GDN_DV = 128
QKV_GDN = 2 * GDN_HEADS * GDN_DK + GDN_HEADS * GDN_DV
GDN_WIDTH = GDN_HEADS * GDN_DV
MLA_WIDTH = MLA_HEADS * V_HEAD
CONV_W = 5
CHUNK = 64
N_EXPERTS = 16
EXPERT_FF = 1024
CAPACITY_FACTOR = 2
EPS = 1e-6

LANE = 128
BF16_ROWS = 16
MIB = 1024 * 1024

NA = GDN_WIDTH + Q_LORA + KV_LORA
NB = QKV_GDN + LANE
QLAT_COL_BLOCK = GDN_WIDTH // Q_LORA
KVLAT_COL_BLOCK = (GDN_WIDTH + Q_LORA) // KV_LORA
MISC_COL_BLOCK = QKV_GDN // LANE
PREP_CHUNKS = 8

NN = (((1,), (0,)), ((), ()))
NT = (((1,), (1,)), ((), ()))
TN = (((0,), (0,)), ((), ()))


def _params(semantics, vmem_mib):
    return pltpu.CompilerParams(dimension_semantics=semantics, vmem_limit_bytes=vmem_mib * MIB)


def _resident(shape):
    nd = len(shape)
    return pl.BlockSpec(shape, lambda *_: (0,) * nd, pipeline_mode=pl.Buffered(1))


def _dot(a, b, dims=NN):
    return lax.dot_general(a, b, dims, preferred_element_type=F32)


def _dotb(a, b, dims=NN):
    return lax.dot_general(a.astype(BF16), b.astype(BF16), dims, preferred_element_type=F32)


def _split(x):
    hi = x.astype(BF16)
    lo = (x - hi.astype(F32)).astype(BF16)
    return hi, lo


def _dot3(a, b, dims=NN):
    ah, al = _split(a)
    bh, bl = _split(b)
    return _dot(ah, bh, dims) + (_dot(ah, bl, dims) + _dot(al, bh, dims))


def _silu(x):
    return x * jax.nn.sigmoid(x)


def _in_proj_kernel(x_ref, g_ref, wa_ref, wb_ref, oa_ref, ob_ref):
    x = x_ref[...]
    ms = jnp.mean(x * x, axis=-1, keepdims=True)
    h = (x * lax.rsqrt(ms + EPS) * g_ref[...]).astype(BF16)
    oa_ref[...] = _dot(h, wa_ref[...]).astype(BF16)
    ob_ref[...] = _dot(h, wb_ref[...])


def _in_proj(x2d, g, wa, wb):
    t = x2d.shape[0]
    tm = min(512, t)
    return pl.pallas_call(
        _in_proj_kernel,
        grid=(t // tm,),
        in_specs=[
            pl.BlockSpec((tm, D_MODEL), lambda i: (i, 0)),
            _resident((1, D_MODEL)),
            _resident(wa.shape),
            _resident(wb.shape),
        ],
        out_specs=[
            pl.BlockSpec((tm, NA), lambda i: (i, 0)),
            pl.BlockSpec((tm, NB), lambda i: (i, 0)),
        ],
        out_shape=[
            jax.ShapeDtypeStruct((t, NA), BF16),
            jax.ShapeDtypeStruct((t, NB), F32),
        ],
        compiler_params=_params(("parallel",), 56),
        name="in_proj",
    )(x2d, g, wa, wb)


def _mla_prep_kernel(ql_ref, kvl_ref, misc_ref, cos_ref, sin_ref, gqa_ref, gkva_ref, wq_ref, wkv_ref,
                     gq_ref, gk_ref, q_ref, k_ref, v_ref):
    def norm(x, g):
        return x * lax.rsqrt(jnp.mean(x * x, axis=-1, keepdims=True) + EPS) * g

    qn = norm(ql_ref[...].astype(F32), gqa_ref[...]).astype(BF16)
    kvn = norm(kvl_ref[...].astype(F32), gkva_ref[...]).astype(BF16)
    q = _dot(qn, wq_ref[...])
    kv = _dot(kvn, wkv_ref[...])

    cos = cos_ref[...]
    sin = sin_ref[...]
    lane = lax.broadcasted_iota(I32, cos.shape, 1)

    def rope(t):
        rot = jnp.where(lane < QK_ROPE // 2, pltpu.roll(t, LANE - QK_ROPE // 2, 1), pltpu.roll(t, QK_ROPE // 2, 1))
        return t * cos + rot * sin

    gq = gq_ref[...]
    gk = gk_ref[...]
    kr = jnp.where(lane < QK_ROPE, misc_ref[...], 0.0)
    kr_ssq = jnp.sum(kr * kr, axis=-1, keepdims=True)
    kr_rot = rope(kr * gk[:, QK_NOPE:])
    q_scale = QK_HEAD ** -0.5
    for h in range(MLA_HEADS):
        qh = q[:, h * QK_PAD:(h + 1) * QK_PAD]
        sc = lax.rsqrt(jnp.sum(qh * qh, axis=-1, keepdims=True) * (1.0 / QK_HEAD) + EPS) * q_scale
        q_ref[0, h, :, :QK_NOPE] = (qh[:, :QK_NOPE] * gq[:, :QK_NOPE] * sc).astype(BF16)
        q_ref[0, h, :, QK_NOPE:] = (rope(qh[:, QK_NOPE:] * gq[:, QK_NOPE:]) * sc).astype(BF16)
        kn = kv[:, h * 256:h * 256 + QK_NOPE]
        sk = lax.rsqrt((jnp.sum(kn * kn, axis=-1, keepdims=True) + kr_ssq) * (1.0 / QK_HEAD) + EPS)
        k_ref[0, h, :, :QK_NOPE] = (kn * gk[:, :QK_NOPE] * sk).astype(BF16)
        k_ref[0, h, :, QK_NOPE:] = (kr_rot * sk).astype(BF16)
        v_ref[0, h] = kv[:, h * 256 + QK_NOPE:(h + 1) * 256].astype(BF16)


def _mla_prep(oa, ob, cos, sin, gqa, gkva, wq, wkv, gq, gk, bt, s):
    tm = min(512, s)
    nt = s // tm
    return pl.pallas_call(
        _mla_prep_kernel,
        grid=(bt, nt),
        in_specs=[
            pl.BlockSpec((tm, Q_LORA), lambda b, i: (b * nt + i, QLAT_COL_BLOCK)),
            pl.BlockSpec((tm, KV_LORA), lambda b, i: (b * nt + i, KVLAT_COL_BLOCK)),
            pl.BlockSpec((tm, LANE), lambda b, i: (b * nt + i, MISC_COL_BLOCK)),
            pl.BlockSpec((tm, LANE), lambda b, i: (i, 0)),
            pl.BlockSpec((tm, LANE), lambda b, i: (i, 0)),
            _resident(gqa.shape),
            _resident(gkva.shape),
            _resident(wq.shape),
            _resident(wkv.shape),
            _resident(gq.shape),
            _resident(gk.shape),
        ],
        out_specs=[
            pl.BlockSpec((1, MLA_HEADS, tm, QK_PAD), lambda b, i: (b, 0, i, 0)),
            pl.BlockSpec((1, MLA_HEADS, tm, QK_PAD), lambda b, i: (b, 0, i, 0)),
            pl.BlockSpec((1, MLA_HEADS, tm, V_HEAD), lambda b, i: (b, 0, i, 0)),
        ],
        out_shape=[
            jax.ShapeDtypeStruct((bt, MLA_HEADS, s, QK_PAD), BF16),
            jax.ShapeDtypeStruct((bt, MLA_HEADS, s, QK_PAD), BF16),
            jax.ShapeDtypeStruct((bt, MLA_HEADS, s, V_HEAD), BF16),
        ],
        compiler_params=_params(("parallel", "parallel"), 48),
        name="mla_prep",
    )(oa, oa, ob, cos, sin, gqa, gkva, wq, wkv, gq, gk)


def _attn_kernel(q_ref, k_ref, v_ref, o_ref):
    s = _dot(q_ref[0, 0], k_ref[0, 0], NT)
    m = jnp.max(s, axis=-1, keepdims=True)
    p = jnp.exp(s - m)
    l = jnp.sum(p, axis=-1, keepdims=True)
    o = _dot(p.astype(BF16), v_ref[0, 0])
    o_ref[0] = (o / l).astype(BF16)


def _attention(q, k, v):
    bt, h, s, _ = q.shape
    tq = min(256, s)
    return pl.pallas_call(
        _attn_kernel,
        grid=(bt, h, s // tq),
        in_specs=[
            pl.BlockSpec((1, 1, tq, QK_PAD), lambda b, hh, i: (b, hh, i, 0)),
            pl.BlockSpec((1, 1, s, QK_PAD), lambda b, hh, i: (b, hh, 0, 0)),
            pl.BlockSpec((1, 1, s, V_HEAD), lambda b, hh, i: (b, hh, 0, 0)),
        ],
        out_specs=pl.BlockSpec((1, tq, V_HEAD), lambda b, hh, i: (b, i, hh)),
        out_shape=jax.ShapeDtypeStruct((bt, s, MLA_WIDTH), BF16),
        compiler_params=_params(("parallel", "parallel", "parallel"), 48),
        name="attention",
    )(q, k, v)


def _conv_kernel(x_ref, w_ref, o_ref):
    c = pl.program_id(1)
    x = x_ref[0]
    s = x.shape[0]
    w = w_ref[...]
    row = lax.broadcasted_iota(I32, x.shape, 0)
    pad = CONV_W // 2
    acc = x * w[pad:pad + 1]
    for d in range(-pad, pad + 1):
        if d == 0:
            continue
        xs = pltpu.roll(x, (-d) % s, 0)
        ok = jnp.logical_and(row + d >= 0, row + d < s)
        acc = acc + jnp.where(ok, xs, 0.0) * w[pad + d:pad + d + 1]
    y = _silu(acc)
    inv = lax.rsqrt(jnp.sum(y * y, axis=-1, keepdims=True) + EPS)
    is_q = c < GDN_HEADS
    is_qk = c < 2 * GDN_HEADS
    scale = jnp.where(is_qk, inv, 1.0) * jnp.where(is_q, GDN_DK ** -0.5, 1.0)
    o_ref[0, 0] = y * scale


def _conv(ob3, conv_w):
    bt, s, _ = ob3.shape
    nblk = QKV_GDN // LANE
    return pl.pallas_call(
        _conv_kernel,
        grid=(bt, nblk),
        in_specs=[
            pl.BlockSpec((1, s, LANE), lambda b, c: (b, 0, c)),
            pl.BlockSpec((CONV_W, LANE), lambda b, c: (0, c)),
        ],
        out_specs=pl.BlockSpec((1, 1, s, LANE), lambda b, c: (b, c, 0, 0)),
        out_shape=jax.ShapeDtypeStruct((bt, nblk, s, LANE), F32),
        compiler_params=_params(("parallel", "parallel"), 48),
        name="gdn_conv",
    )(ob3, conv_w)


def _gates_kernel(x_ref, alog_ref, dtb_ref, o_ref):
    x = x_ref[0]
    nh = GDN_HEADS
    a = x[:2 * nh] + dtb_ref[...]
    softplus = jnp.maximum(a, 0.0) + jnp.log1p(jnp.exp(-jnp.abs(a)))
    g = -jnp.exp(alog_ref[...]) * softplus
    beta = jax.nn.sigmoid(x[2 * nh:])
    gf = g[:nh]
    gb = g[nh:]
    pos = lax.broadcasted_iota(I32, gf.shape, 2) % CHUNK
    sh = 1
    while sh < CHUNK:
        gf = gf + jnp.where(pos >= sh, pltpu.roll(gf, sh, 2), 0.0)
        gb = gb + jnp.where(pos < CHUNK - sh, pltpu.roll(gb, LANE - sh, 2), 0.0)
        sh *= 2
    o_ref[0, :nh] = gf
    o_ref[0, nh:2 * nh] = gb
    o_ref[0, 2 * nh:] = beta


def _gates(gt, alog, dtb):
    bt, r, sl, _ = gt.shape
    return pl.pallas_call(
        _gates_kernel,
        grid=(bt,),
        in_specs=[
            pl.BlockSpec((1, r, sl, LANE), lambda b: (b, 0, 0, 0)),
            _resident(alog.shape),
            _resident(dtb.shape),
        ],
        out_specs=pl.BlockSpec((1, r, sl, LANE), lambda b: (b, 0, 0, 0)),
        out_shape=jax.ShapeDtypeStruct(gt.shape, F32),
        compiler_params=_params(("parallel",), 32),
        name="gdn_gates",
    )(gt, alog, dtb)


def _unit_tri_inverses(lmats, eye):
    ps = [eye - l for l in lmats]
    ms = [_dotb(l, l) for l in lmats]
    n = 2
    while n < CHUNK // 2:
        ps = [p + _dotb(p, m) for p, m in zip(ps, ms)]
        ms = [_dotb(m, m) for m in ms]
        n *= 2
    ps = [p + _dotb(p, m) for p, m in zip(ps, ms)]
    resid = [eye - (p + _dot3(l, p)) for l, p in zip(lmats, ps)]
    return [p + _dotb(p, r) for p, r in zip(ps, resid)]


def _gdn_prep_kernel(q_ref, k_ref, v_ref, grf_ref, grb_ref, gc_ref,
                     wqf_ref, uf_ref, kdf_ref, inf_ref, wqb_ref, ub_ref, kdb_ref, inb_ref):
    ri = lax.broadcasted_iota(I32, (CHUNK, CHUNK), 0)
    ci = lax.broadcasted_iota(I32, (CHUNK, CHUNK), 1)
    eye = (ri == ci).astype(F32)
    dirs = (
        (0, grf_ref, ri >= ci, ri > ci, CHUNK - 1, (wqf_ref, uf_ref, kdf_ref, inf_ref)),
        (1, grb_ref, ri <= ci, ri < ci, 0, (wqb_ref, ub_ref, kdb_ref, inb_ref)),
    )
    lmats = []
    work = []
    for c in range(PREP_CHUNKS):
        rows = slice(c * CHUNK, (c + 1) * CHUNK)
        q = q_ref[0, 0, rows, :]
        k = k_ref[0, 0, rows, :]
        v = v_ref[0, 0, rows, :]
        g4 = gc_ref[0, 0, rows, :]
        kk = _dotb(k, k, NT)
        qk = _dotb(q, k, NT)
        for col, grow_ref, incl, strict, last_row, outs in dirs:
            gcol = g4[:, col:col + 1]
            beta = g4[:, 2 + col:3 + col]
            decay = jnp.exp(jnp.where(incl, gcol - grow_ref[0, 0, c:c + 1, :], -jnp.inf))
            lmats.append(jnp.where(strict, beta * kk * decay, 0.0))
            e_gc = jnp.exp(gcol)
            rhs = jnp.concatenate([v * beta, k * (beta * e_gc)], axis=-1)
            k_dec = k * jnp.exp(g4[last_row:last_row + 1, col:col + 1] - gcol)
            intra = jnp.where(incl, qk * decay, 0.0)
            work.append((c, rows, rhs, q * e_gc, k_dec, intra, outs))
    tinvs = _unit_tri_inverses(lmats, eye)
    for tinv, (c, rows, rhs, q_dec, k_dec, intra, (wq_ref, u_ref, kd_ref, in_ref)) in zip(tinvs, work):
        sol = _dot3(tinv, rhs)
        u_ref[0, 0, rows, :] = sol[:, :GDN_DV]
        wq_ref[0, 0, c, :CHUNK, :] = sol[:, GDN_DV:].astype(BF16)
        wq_ref[0, 0, c, CHUNK:, :] = q_dec.astype(BF16)
        kd_ref[0, 0, rows, :] = k_dec.astype(BF16)
        in_ref[0, 0, rows, :] = intra.astype(BF16)


def _gdn_prep(qkv_h, g_rows, g_cols):
    bt, _, s, _ = qkv_h.shape
    nh = GDN_HEADS
    rb = PREP_CHUNKS * CHUNK
    n_chunks = s // CHUNK
    head = lambda off: (lambda b, h, i: (b, off + h, i, 0))
    seq_spec = lambda w: pl.BlockSpec((1, 1, rb, w), head(0))
    wq_spec = pl.BlockSpec((1, 1, PREP_CHUNKS, 2 * CHUNK, GDN_DK), lambda b, h, i: (b, h, i, 0, 0))
    out_specs = [wq_spec, seq_spec(GDN_DV), seq_spec(GDN_DK), seq_spec(CHUNK)]
    out_shape = [
        jax.ShapeDtypeStruct((bt, nh, n_chunks, 2 * CHUNK, GDN_DK), BF16),
        jax.ShapeDtypeStruct((bt, nh, s, GDN_DV), F32),
        jax.ShapeDtypeStruct((bt, nh, s, GDN_DK), BF16),
        jax.ShapeDtypeStruct((bt, nh, s, CHUNK), BF16),
    ]
    return pl.pallas_call(
        _gdn_prep_kernel,
        grid=(bt, nh, s // rb),
        in_specs=[
            pl.BlockSpec((1, 1, rb, GDN_DK), head(0)),
            pl.BlockSpec((1, 1, rb, GDN_DK), head(nh)),
            pl.BlockSpec((1, 1, rb, GDN_DV), head(2 * nh)),
            pl.BlockSpec((1, 1, PREP_CHUNKS, CHUNK), head(0)),
            pl.BlockSpec((1, 1, PREP_CHUNKS, CHUNK), head(nh)),
            pl.BlockSpec((1, 1, rb, 4), head(0)),
        ],
        out_specs=out_specs + out_specs,
        out_shape=out_shape + out_shape,
        compiler_params=_params(("parallel", "parallel", "parallel"), 48),
        name="gdn_prep",
    )(qkv_h, qkv_h, qkv_h, g_rows, g_rows, g_cols)


def _gdn_scan_kernel(wqf_ref, uf_ref, kdf_ref, inf_ref, grf_ref, wqb_ref, ub_ref, kdb_ref, inb_ref, grb_ref,
                     of_ref, ob_ref, state_ref):
    @pl.when(pl.program_id(1) == 0)
    def _():
        state_ref[...] = jnp.zeros_like(state_ref)

    def step(c, carry):
        dirs = (
            (0, c, wqf_ref, uf_ref, kdf_ref, inf_ref, grf_ref, CHUNK - 1, of_ref),
            (1, PREP_CHUNKS - 1 - c, wqb_ref, ub_ref, kdb_ref, inb_ref, grb_ref, 0, ob_ref),
        )
        chains = [(h,) + d for h in range(GDN_HEADS) for d in dirs]
        states = [state_ref[d, h] for h, d, *_ in chains]
        rs = [_dot(wq_ref[0, h, cc], s.astype(BF16)) for (h, _, cc, wq_ref, *_), s in zip(chains, states)]
        for (h, d, cc, _, u_ref, kd_ref, in_ref, g_ref, last_lane, o_ref), s, r in zip(chains, states, rs):
            rows = pl.ds(pl.multiple_of(cc * CHUNK, CHUNK), CHUNK)
            v_new = (u_ref[0, h, rows, :] - r[:CHUNK]).astype(BF16)
            o_ref[0, rows, h * GDN_DV:(h + 1) * GDN_DV] = r[CHUNK:] + _dot(in_ref[0, h, rows, :], v_new)
            g_last = g_ref[0, d * GDN_HEADS + h, pl.ds(cc, 1), :][:, last_lane:last_lane + 1]
            state_ref[d, h] = s * jnp.exp(g_last) + _dot(kd_ref[0, h, rows, :], v_new, TN)
        return carry

    lax.fori_loop(0, PREP_CHUNKS, step, 0)


def _gdn_scan(prep, g_rows):
    wqf, uf, kdf, inf_, wqb, ub, kdb, inb = prep
    bt, nh, s, _ = uf.shape
    rb = PREP_CHUNKS * CHUNK
    nblk = s // rb
    fwd = lambda b, i: (b, 0, i, 0)
    bwd = lambda b, i: (b, 0, nblk - 1 - i, 0)

    def specs(idx):
        idx5 = lambda b, i: idx(b, i) + (0,)
        return [
            pl.BlockSpec((1, nh, PREP_CHUNKS, 2 * CHUNK, GDN_DK), idx5),
            pl.BlockSpec((1, nh, rb, GDN_DV), idx),
            pl.BlockSpec((1, nh, rb, GDN_DK), idx),
            pl.BlockSpec((1, nh, rb, CHUNK), idx),
            pl.BlockSpec((1, 2 * nh, PREP_CHUNKS, CHUNK), idx),
        ]

    out = jax.ShapeDtypeStruct((bt, s, GDN_WIDTH), F32)
    return pl.pallas_call(
        _gdn_scan_kernel,
        grid=(bt, nblk),
        in_specs=specs(fwd) + specs(bwd),
        out_specs=[
            pl.BlockSpec((1, rb, GDN_WIDTH), lambda b, i: (b, i, 0)),
            pl.BlockSpec((1, rb, GDN_WIDTH), lambda b, i: (b, nblk - 1 - i, 0)),
        ],
        out_shape=[out, out],
        scratch_shapes=[pltpu.VMEM((2, nh, GDN_DK, GDN_DV), F32)],
        compiler_params=_params(("parallel", "arbitrary"), 48),
        name="gdn_scan",
    )(wqf, uf, kdf, inf_, g_rows, wqb, ub, kdb, inb, g_rows)


def _out_proj_kernel(x_ref, mla_ref, of_ref, ob_ref, z_ref, gn_ref, wo_ref, g_ref, wr_ref, x1_ref, xn_ref, aff_ref,
                     gdn_ref):
    for h in range(GDN_HEADS):
        cols = slice(h * GDN_DV, (h + 1) * GDN_DV)
        o = of_ref[:, cols] + ob_ref[:, cols]
        on = o * lax.rsqrt(jnp.mean(o * o, axis=-1, keepdims=True) + EPS) * gn_ref[...]
        gdn_ref[:, cols] = (on * _silu(z_ref[:, cols].astype(F32))).astype(BF16)
    y = x_ref[...] + _dot(mla_ref[...], wo_ref[:MLA_WIDTH, :]) + _dot(gdn_ref[...], wo_ref[MLA_WIDTH:, :])
    x1_ref[...] = y
    xn = y * lax.rsqrt(jnp.mean(y * y, axis=-1, keepdims=True) + EPS) * g_ref[...]
    xn_ref[...] = xn
    logits = _dot3(xn, wr_ref[...])
    e = jnp.exp(logits - jnp.max(logits, axis=-1, keepdims=True))
    aff_ref[...] = e / jnp.sum(e, axis=-1, keepdims=True)


def _out_proj(x2d, mla, o_fwd, o_bwd, oa, gn, wo, g2, wr):
    t = x2d.shape[0]
    tm = min(256, t)
    rows = lambda w: pl.BlockSpec((tm, w), lambda i: (i, 0))
    return pl.pallas_call(
        _out_proj_kernel,
        grid=(t // tm,),
        in_specs=[rows(D_MODEL), rows(MLA_WIDTH), rows(GDN_WIDTH), rows(GDN_WIDTH), rows(GDN_WIDTH),
                  _resident(gn.shape), _resident(wo.shape), _resident(g2.shape), _resident(wr.shape)],
        out_specs=[rows(D_MODEL), rows(D_MODEL), rows(N_EXPERTS)],
        out_shape=[
            jax.ShapeDtypeStruct((t, D_MODEL), F32),
            jax.ShapeDtypeStruct((t, D_MODEL), F32),
            jax.ShapeDtypeStruct((t, N_EXPERTS), F32),
        ],
        scratch_shapes=[pltpu.VMEM((tm, GDN_WIDTH), BF16)],
        compiler_params=_params(("parallel",), 48),
        name="out_proj",
    )(x2d, mla, o_fwd, o_bwd, oa, gn, wo, g2, wr)


def _ffn_kernel(idx_ref, xn_hbm, gate_ref, wg_ref, wu_ref, wd_ref, ye_ref, xe32_ref, xe_ref, acc_ref, sem):
    f = pl.program_id(1)
    cap = xe_ref.shape[0]

    def row_copy(slot):
        tok = idx_ref[0, 0, slot]
        return pltpu.make_async_copy(xn_hbm.at[pl.ds(tok, 1), :], xe32_ref.at[pl.ds(slot, 1), :], sem.at[0])

    @pl.when(f == 0)
    def _():
        def start(slot, c):
            row_copy(slot).start()
            return c

        def wait(slot, c):
            row_copy(slot).wait()
            return c

        lax.fori_loop(0, cap, start, 0)
        lax.fori_loop(0, cap, wait, 0)
        xe_ref[...] = xe32_ref[...].astype(BF16)
        acc_ref[...] = jnp.zeros_like(acc_ref)

    xe = xe_ref[...]
    g = _dot(xe, wg_ref[0].astype(BF16))
    u = _dot(xe, wu_ref[0].astype(BF16))
    h = (_silu(g) * u).astype(BF16)
    acc_ref[...] += _dot(h, wd_ref[0].astype(BF16))

    @pl.when(f == pl.num_programs(1) - 1)
    def _():
        ye_ref[0] = (acc_ref[...] * gate_ref[0]).astype(BF16)


def _ffn(idx3, xn, gates3, w_gate, w_up, w_down):
    e, _, cap = idx3.shape
    tf = 256
    return pl.pallas_call(
        _ffn_kernel,
        grid=(e, EXPERT_FF // tf),
        in_specs=[
            pl.BlockSpec((1, 1, cap), lambda ee, f: (ee, 0, 0), memory_space=pltpu.SMEM),
            pl.BlockSpec(memory_space=pl.ANY),
            pl.BlockSpec((1, cap, 1), lambda ee, f: (ee, 0, 0)),
            pl.BlockSpec((1, D_MODEL, tf), lambda ee, f: (ee, 0, f)),
            pl.BlockSpec((1, D_MODEL, tf), lambda ee, f: (ee, 0, f)),
            pl.BlockSpec((1, tf, D_MODEL), lambda ee, f: (ee, f, 0)),
        ],
        out_specs=pl.BlockSpec((1, cap, D_MODEL), lambda ee, f: (ee, 0, 0)),
        out_shape=jax.ShapeDtypeStruct((e, cap, D_MODEL), BF16),
        scratch_shapes=[
            pltpu.VMEM((cap, D_MODEL), F32),
            pltpu.VMEM((cap, D_MODEL), BF16),
            pltpu.VMEM((cap, D_MODEL), F32),
            pltpu.SemaphoreType.DMA((1,)),
        ],
        compiler_params=_params(("arbitrary", "arbitrary"), 56),
        name="moe_ffn",
    )(idx3, xn, gates3, w_gate, w_up, w_down)


def _combine_kernel(lo_ref, slot_ref, x1_ref, ye_hbm, y_ref, win_ref, sem):
    i = pl.program_id(0)
    nt = pl.num_programs(0)
    n_e = win_ref.shape[1]
    width = win_ref.shape[2]

    def window_copy(tile, buf, e):
        lo = pl.multiple_of(lo_ref[tile * n_e + e], BF16_ROWS)
        return pltpu.make_async_copy(ye_hbm.at[e, pl.ds(lo, width), :], win_ref.at[buf, e], sem.at[buf, e])

    @pl.when(i == 0)
    def _():
        for e in range(n_e):
            window_copy(0, 0, e).start()

    @pl.when(i + 1 < nt)
    def _():
        for e in range(n_e):
            window_copy(i + 1, (i + 1) % 2, e).start()

    buf = i % 2
    slot = slot_ref[...]
    acc = x1_ref[...]
    col = lax.broadcasted_iota(I32, (slot.shape[0], width), 1)
    for e in range(n_e):
        window_copy(i, buf, e).wait()
        rel = slot[:, e:e + 1] - lo_ref[i * n_e + e]
        onehot = (rel == col).astype(BF16)
        acc = acc + _dot(onehot, win_ref[buf, e])
    y_ref[...] = acc


def _combine(lo_flat, slot, x1, ye, tt):
    t = x1.shape[0]
    n_e = ye.shape[0]
    width = tt + BF16_ROWS
    grid_spec = pltpu.PrefetchScalarGridSpec(
        num_scalar_prefetch=1,
        grid=(t // tt,),
        in_specs=[
            pl.BlockSpec((tt, n_e), lambda i, lo: (i, 0)),
            pl.BlockSpec((tt, D_MODEL), lambda i, lo: (i, 0)),
            pl.BlockSpec(memory_space=pl.ANY),
        ],
        out_specs=pl.BlockSpec((tt, D_MODEL), lambda i, lo: (i, 0)),
        scratch_shapes=[
            pltpu.VMEM((2, n_e, width, D_MODEL), BF16),
            pltpu.SemaphoreType.DMA((2, n_e)),
        ],
    )
    return pl.pallas_call(
        _combine_kernel,
        grid_spec=grid_spec,
        out_shape=jax.ShapeDtypeStruct((t, D_MODEL), F32),
        compiler_params=_params(("arbitrary",), 48),
        name="moe_combine",
    )(lo_flat, slot, x1, ye)


def _rope_tables(s):
    half = QK_ROPE // 2
    inv_freq = ROPE_THETA ** (-jnp.arange(half, dtype=F32) / half)
    ang = jnp.arange(s, dtype=F32)[:, None] * inv_freq[None, :]
    cos = jnp.cos(ang)
    sin = jnp.sin(ang)
    zeros = jnp.zeros((s, LANE - QK_ROPE), F32)
    return jnp.concatenate([cos, cos, zeros], axis=1), jnp.concatenate([-sin, sin, zeros], axis=1)


def _prepare_weights(norm1_g, w_in, q_a_norm_g, w_q_b, kv_a_norm_g, w_kv_b, q_norm_g, k_norm_g, conv_w,
                     a_log_fwd, a_log_bwd, dt_bias_fwd, dt_bias_bwd, gdn_norm_g, w_o, norm2_g, w_router):
    o_kv = Q_LORA
    o_kr = o_kv + KV_LORA
    o_qkv = o_kr + QK_ROPE
    o_z = o_qkv + QKV_GDN
    o_g = o_z + GDN_WIDTH
    wa = jnp.concatenate([w_in[:, o_z:o_g], w_in[:, :o_kr]], axis=1).astype(BF16)
    wb = jnp.concatenate(
        [w_in[:, o_qkv:o_z], w_in[:, o_kr:o_qkv], w_in[:, o_g:], jnp.zeros((D_MODEL, LANE - QK_ROPE - 4 * GDN_HEADS), F32)],
        axis=1).astype(BF16)
    wq = jnp.pad(w_q_b.reshape(Q_LORA, MLA_HEADS, QK_HEAD), ((0, 0), (0, 0), (0, QK_PAD - QK_HEAD)))
    wq = wq.reshape(Q_LORA, MLA_HEADS * QK_PAD).astype(BF16)
    pad_g = lambda g: jnp.pad(g, (0, QK_PAD - QK_HEAD)).reshape(1, QK_PAD)
    alog = jnp.concatenate([a_log_fwd, a_log_bwd]).reshape(2 * GDN_HEADS, 1, 1)
    dtb = jnp.concatenate([dt_bias_fwd, dt_bias_bwd]).reshape(2 * GDN_HEADS, 1, 1)
    return dict(
        g1=norm1_g.reshape(1, D_MODEL), wa=wa, wb=wb,
        gqa=q_a_norm_g.reshape(1, Q_LORA), gkva=kv_a_norm_g.reshape(1, KV_LORA),
        wq=wq, wkv=w_kv_b.astype(BF16), gq=pad_g(q_norm_g), gk=pad_g(k_norm_g),
        conv_w=conv_w,
        alog=jnp.broadcast_to(alog, (2 * GDN_HEADS, 1, LANE)), dtb=jnp.broadcast_to(dtb, (2 * GDN_HEADS, 1, LANE)),
        gn=gdn_norm_g.reshape(1, GDN_DV), wo=w_o.astype(BF16), g2=norm2_g.reshape(1, D_MODEL), wr=w_router,
    )


def _route(aff, cap, tt):
    t, n_e = aff.shape
    gates, idx = lax.top_k(aff.T, cap)
    order = jnp.argsort(idx, axis=1)
    idx = jnp.take_along_axis(idx, order, axis=1)
    gates = jnp.take_along_axis(gates, order, axis=1)
    e_ids = jnp.broadcast_to(jnp.arange(n_e, dtype=I32)[:, None], idx.shape)
    s_ids = jnp.broadcast_to(jnp.arange(cap, dtype=I32)[None, :], idx.shape)
    slot = jnp.full((t, n_e), -1, I32).at[idx, e_ids].set(s_ids)
    per_tile = (slot >= 0).reshape(t // tt, tt, n_e).sum(axis=1).astype(I32)
    lo = jnp.cumsum(per_tile, axis=0) - per_tile
    lo = jnp.minimum((lo // BF16_ROWS) * BF16_ROWS, cap - (tt + BF16_ROWS))
    return idx.astype(I32), gates, slot, lo.reshape(-1)


def _encoder_layer(x, p, w_gate, w_up, w_down):
    bt, s, _ = x.shape
    t = bt * s
    x2d = x.reshape(t, D_MODEL)
    oa, ob = _in_proj(x2d, p["g1"], p["wa"], p["wb"])

    cos, sin = _rope_tables(s)
    q, k, v = _mla_prep(oa, ob, cos, sin, p["gqa"], p["gkva"], p["wq"], p["wkv"], p["gq"], p["gk"], bt, s)
    mla_out = _attention(q, k, v)

    ob3 = ob.reshape(bt, s, NB)
    qkv_h = _conv(ob3, p["conv_w"])
    g_raw = ob3[:, :, QKV_GDN + QK_ROPE:QKV_GDN + QK_ROPE + 4 * GDN_HEADS]
    g_t = jnp.swapaxes(g_raw, 1, 2).reshape(bt, 4 * GDN_HEADS, s // LANE, LANE)
    g_out = _gates(g_t, p["alog"], p["dtb"]).reshape(bt, 4, GDN_HEADS, s)
    g_rows = g_out[:, :2].reshape(bt, 2 * GDN_HEADS, s // CHUNK, CHUNK)
    g_cols = jnp.transpose(g_out, (0, 2, 3, 1))
    o_fwd, o_bwd = _gdn_scan(_gdn_prep(qkv_h, g_rows, g_cols), g_rows)

    x1, xn, aff = _out_proj(x2d, mla_out.reshape(t, MLA_WIDTH), o_fwd.reshape(t, GDN_WIDTH),
                            o_bwd.reshape(t, GDN_WIDTH), oa, p["gn"], p["wo"], p["g2"], p["wr"])

    cap = max(1, CAPACITY_FACTOR * t // N_EXPERTS)
    tt = 128
    idx, gates, slot, lo = _route(aff, cap, tt)
    ye = _ffn(idx.reshape(N_EXPERTS, 1, cap), xn, gates.reshape(N_EXPERTS, cap, 1), w_gate, w_up, w_down)
    y = _combine(lo, slot, x1, ye, tt)
    return y.reshape(bt, s, D_MODEL)


def kernel(x_prompt, x_sample, norm1_g, w_in, q_a_norm_g, w_q_b, kv_a_norm_g, w_kv_b, q_norm_g, k_norm_g, conv_w,
           a_log_fwd, a_log_bwd, dt_bias_fwd, dt_bias_bwd, gdn_norm_g, w_o, norm2_g, w_router, w_gate, w_up, w_down):
    y_prompt = x_prompt
    y_sample = x_sample
    for l in range(norm1_g.shape[0]):
        p = _prepare_weights(norm1_g[l], w_in[l], q_a_norm_g[l], w_q_b[l], kv_a_norm_g[l], w_kv_b[l], q_norm_g[l],
                             k_norm_g[l], conv_w[l], a_log_fwd[l], a_log_bwd[l], dt_bias_fwd[l], dt_bias_bwd[l],
                             gdn_norm_g[l], w_o[l], norm2_g[l], w_router[l])
        y_prompt = _encoder_layer(y_prompt, p, w_gate[l], w_up[l], w_down[l])
        y_sample = _encoder_layer(y_sample, p, w_gate[l], w_up[l], w_down[l])
    return (y_prompt, y_sample)
```

```python
import functools
import math

import jax
import jax.numpy as jnp
from jax import lax
from jax.experimental import pallas as pl
from jax.experimental.pallas import tpu as pltpu

F32 = jnp.float32
BF16 = jnp.bfloat16
I32 = jnp.int32

D_MODEL = 2048
MLA_HEADS = 8
Q_LORA = 512
KV_LORA = 256
QK_NOPE = 128
QK_ROPE = 64
QK_HEAD = QK_NOPE + QK_ROPE
QK_PAD = 256
V_HEAD = 128
V_PAD = 256
ROPE_THETA = 10000.0
GDN_HEADS = 8
GDN_DK = 128
GDN_DV = 128
QKV_GDN = 2 * GDN_HEADS * GDN_DK + GDN_HEADS * GDN_DV
GDN_WIDTH = GDN_HEADS * GDN_DV
MLA_WIDTH = MLA_HEADS * V_HEAD
CONV_W = 5
CHUNK = 64
N_EXPERTS = 16
EXPERT_FF = 1024
CAPACITY_FACTOR = 2
EPS = 1e-6

LANE = 128
BF16_ROWS = 16
MIB = 1024 * 1024

NA = GDN_WIDTH + Q_LORA + KV_LORA
NB = QKV_GDN + LANE
QLAT_COL_BLOCK = GDN_WIDTH // Q_LORA
KVLAT_COL_BLOCK = (GDN_WIDTH + Q_LORA) // KV_LORA
MISC_COL_BLOCK = QKV_GDN // LANE
PREP_CHUNKS = 8

NN = (((1,), (0,)), ((), ()))
NT = (((1,), (1,)), ((), ()))
TN = (((0,), (0,)), ((), ()))


def _params(semantics, vmem_mib):
    return pltpu.CompilerParams(dimension_semantics=semantics, vmem_limit_bytes=vmem_mib * MIB)


def _resident(shape):
    nd = len(shape)
    return pl.BlockSpec(shape, lambda *_: (0,) * nd, pipeline_mode=pl.Buffered(1))


def _dot(a, b, dims=NN):
    return lax.dot_general(a, b, dims, preferred_element_type=F32)


def _dotb(a, b, dims=NN):
    return lax.dot_general(a.astype(BF16), b.astype(BF16), dims, preferred_element_type=F32)


def _split(x):
    hi = x.astype(BF16)
    lo = (x - hi.astype(F32)).astype(BF16)
    return hi, lo


def _dot3(a, b, dims=NN):
    ah, al = _split(a)
    bh, bl = _split(b)
    return _dot(ah, bh, dims) + (_dot(ah, bl, dims) + _dot(al, bh, dims))


def _silu(x):
    return x * jax.nn.sigmoid(x)


def _in_proj_kernel(x_ref, g_ref, wa_ref, wb_ref, oa_ref, ob_ref):
    x = x_ref[...]
    ms = jnp.mean(x * x, axis=-1, keepdims=True)
    h = (x * lax.rsqrt(ms + EPS) * g_ref[...]).astype(BF16)
    oa_ref[...] = _dot(h, wa_ref[...]).astype(BF16)
    ob_ref[...] = _dot(h, wb_ref[...])


def _in_proj(x2d, g, wa, wb):
    t = x2d.shape[0]
    tm = min(512, t)
    return pl.pallas_call(
        _in_proj_kernel,
        grid=(t // tm,),
        in_specs=[
            pl.BlockSpec((tm, D_MODEL), lambda i: (i, 0)),
            _resident((1, D_MODEL)),
            _resident(wa.shape),
            _resident(wb.shape),
        ],
        out_specs=[
            pl.BlockSpec((tm, NA), lambda i: (i, 0)),
            pl.BlockSpec((tm, NB), lambda i: (i, 0)),
        ],
        out_shape=[
            jax.ShapeDtypeStruct((t, NA), BF16),
            jax.ShapeDtypeStruct((t, NB), F32),
        ],
        compiler_params=_params(("parallel",), 56),
        name="in_proj",
    )(x2d, g, wa, wb)


def _mla_prep_kernel(ql_ref, kvl_ref, misc_ref, cos_ref, sin_ref, gqa_ref, gkva_ref, wq_ref, wkv_ref,
                     gq_ref, gk_ref, q_ref, k_ref, v_ref):
    def norm(x, g):
        return x * lax.rsqrt(jnp.mean(x * x, axis=-1, keepdims=True) + EPS) * g

    qn = norm(ql_ref[...].astype(F32), gqa_ref[...]).astype(BF16)
    kvn = norm(kvl_ref[...].astype(F32), gkva_ref[...]).astype(BF16)
    q = _dot(qn, wq_ref[...])
    kv = _dot(kvn, wkv_ref[...])

    cos = cos_ref[...]
    sin = sin_ref[...]
    lane = lax.broadcasted_iota(I32, cos.shape, 1)

    def rope(t):
        rot = jnp.where(lane < QK_ROPE // 2, pltpu.roll(t, LANE - QK_ROPE // 2, 1), pltpu.roll(t, QK_ROPE // 2, 1))
        return t * cos + rot * sin

    gq = gq_ref[...]
    gk = gk_ref[...]
    kr = jnp.where(lane < QK_ROPE, misc_ref[...], 0.0)
    kr_ssq = jnp.sum(kr * kr, axis=-1, keepdims=True)
    kr_rot = rope(kr * gk[:, QK_NOPE:])
    q_scale = QK_HEAD ** -0.5
    ones_col = jnp.where(lane == 0, 1.0, 0.0).astype(BF16)
    for h in range(MLA_HEADS):
        qh = q[:, h * QK_PAD:(h + 1) * QK_PAD]
        sc = lax.rsqrt(jnp.sum(qh * qh, axis=-1, keepdims=True) * (1.0 / QK_HEAD) + EPS) * q_scale
        q_ref[0, h, :, :QK_NOPE] = (qh[:, :QK_NOPE] * gq[:, :QK_NOPE] * sc).astype(BF16)
        q_ref[0, h, :, QK_NOPE:] = (rope(qh[:, QK_NOPE:] * gq[:, QK_NOPE:]) * sc).astype(BF16)
        kn = kv[:, h * 256:h * 256 + QK_NOPE]
        sk = lax.rsqrt((jnp.sum(kn * kn, axis=-1, keepdims=True) + kr_ssq) * (1.0 / QK_HEAD) + EPS)
        k_ref[0, h, :, :QK_NOPE] = (kn * gk[:, :QK_NOPE] * sk).astype(BF16)
        k_ref[0, h, :, QK_NOPE:] = (kr_rot * sk).astype(BF16)
        v_ref[0, h, :, :V_HEAD] = kv[:, h * 256 + QK_NOPE:(h + 1) * 256].astype(BF16)
        v_ref[0, h, :, V_HEAD:] = ones_col


def _mla_prep(oa, ob, cos, sin, gqa, gkva, wq, wkv, gq, gk, bt, s):
    tm = min(512, s)
    nt = s // tm
    return pl.pallas_call(
        _mla_prep_kernel,
        grid=(bt, nt),
        in_specs=[
            pl.BlockSpec((tm, Q_LORA), lambda b, i: (b * nt + i, QLAT_COL_BLOCK)),
            pl.BlockSpec((tm, KV_LORA), lambda b, i: (b * nt + i, KVLAT_COL_BLOCK)),
            pl.BlockSpec((tm, LANE), lambda b, i: (b * nt + i, MISC_COL_BLOCK)),
            pl.BlockSpec((tm, LANE), lambda b, i: (i, 0)),
            pl.BlockSpec((tm, LANE), lambda b, i: (i, 0)),
            _resident(gqa.shape),
            _resident(gkva.shape),
            _resident(wq.shape),
            _resident(wkv.shape),
            _resident(gq.shape),
            _resident(gk.shape),
        ],
        out_specs=[
            pl.BlockSpec((1, MLA_HEADS, tm, QK_PAD), lambda b, i: (b, 0, i, 0)),
            pl.BlockSpec((1, MLA_HEADS, tm, QK_PAD), lambda b, i: (b, 0, i, 0)),
            pl.BlockSpec((1, MLA_HEADS, tm, V_PAD), lambda b, i: (b, 0, i, 0)),
        ],
        out_shape=[
            jax.ShapeDtypeStruct((bt, MLA_HEADS, s, QK_PAD), BF16),
            jax.ShapeDtypeStruct((bt, MLA_HEADS, s, QK_PAD), BF16),
            jax.ShapeDtypeStruct((bt, MLA_HEADS, s, V_PAD), BF16),
        ],
        compiler_params=_params(("parallel", "parallel"), 48),
        name="mla_prep",
    )(oa, oa, ob, cos, sin, gqa, gkva, wq, wkv, gq, gk)


def _attn_kernel(q_ref, k_ref, v_ref, o_ref):
    k = k_ref[0, 0]
    v = v_ref[0, 0]
    half = q_ref.shape[2] // 2
    halves = (slice(0, half), slice(half, 2 * half))
    scores = [_dot(q_ref[0, 0, rows, :], k, NT) for rows in halves]
    for rows, s in zip(halves, scores):
        m = jnp.max(s, axis=-1, keepdims=True)
        p = jnp.exp((s - m).astype(BF16))
        o = _dot(p, v)
        o_ref[0, rows, :] = (o[:, :V_HEAD] / o[:, V_HEAD:V_HEAD + 1]).astype(BF16)


def _attention(q, k, v):
    bt, h, s, _ = q.shape
    tq = min(512, s)
    return pl.pallas_call(
        _attn_kernel,
        grid=(bt, h, s // tq),
        in_specs=[
            pl.BlockSpec((1, 1, tq, QK_PAD), lambda b, hh, i: (b, hh, i, 0)),
            pl.BlockSpec((1, 1, s, QK_PAD), lambda b, hh, i: (b, hh, 0, 0)),
            pl.BlockSpec((1, 1, s, V_PAD), lambda b, hh, i: (b, hh, 0, 0)),
        ],
        out_specs=pl.BlockSpec((1, tq, V_HEAD), lambda b, hh, i: (b, i, hh)),
        out_shape=jax.ShapeDtypeStruct((bt, s, MLA_WIDTH), BF16),
        compiler_params=_params(("parallel", "parallel", "parallel"), 48),
        name="attention",
    )(q, k, v)


def _conv_kernel(x_ref, w_ref, o_ref):
    c = pl.program_id(1)
    x = x_ref[0]
    s = x.shape[0]
    w = w_ref[...]
    row = lax.broadcasted_iota(I32, x.shape, 0)
    pad = CONV_W // 2
    acc = x * w[pad:pad + 1]
    for d in range(-pad, pad + 1):
        if d == 0:
            continue
        xs = pltpu.roll(x, (-d) % s, 0)
        ok = jnp.logical_and(row + d >= 0, row + d < s)
        acc = acc + jnp.where(ok, xs, 0.0) * w[pad + d:pad + d + 1]
    y = _silu(acc)
    inv = lax.rsqrt(jnp.sum(y * y, axis=-1, keepdims=True) + EPS)
    is_q = c < GDN_HEADS
    is_qk = c < 2 * GDN_HEADS
    scale = jnp.where(is_qk, inv, 1.0) * jnp.where(is_q, GDN_DK ** -0.5, 1.0)
    o_ref[0, 0] = y * scale


def _conv(ob3, conv_w):
    bt, s, _ = ob3.shape
    nblk = QKV_GDN // LANE
    return pl.pallas_call(
        _conv_kernel,
        grid=(bt, nblk),
        in_specs=[
            pl.BlockSpec((1, s, LANE), lambda b, c: (b, 0, c)),
            pl.BlockSpec((CONV_W, LANE), lambda b, c: (0, c)),
        ],
        out_specs=pl.BlockSpec((1, 1, s, LANE), lambda b, c: (b, c, 0, 0)),
        out_shape=jax.ShapeDtypeStruct((bt, nblk, s, LANE), F32),
        compiler_params=_params(("parallel", "parallel"), 48),
        name="gdn_conv",
    )(ob3, conv_w)


def _gates_kernel(x_ref, alog_ref, dtb_ref, o_ref):
    x = x_ref[0]
    nh = GDN_HEADS
    a = x[:2 * nh] + dtb_ref[...]
    softplus = jnp.maximum(a, 0.0) + jnp.log1p(jnp.exp(-jnp.abs(a)))
    g = -jnp.exp(alog_ref[...]) * softplus
    beta = jax.nn.sigmoid(x[2 * nh:])
    gf = g[:nh]
    gb = g[nh:]
    pos = lax.broadcasted_iota(I32, gf.shape, 2) % CHUNK
    sh = 1
    while sh < CHUNK:
        gf = gf + jnp.where(pos >= sh, pltpu.roll(gf, sh, 2), 0.0)
        gb = gb + jnp.where(pos < CHUNK - sh, pltpu.roll(gb, LANE - sh, 2), 0.0)
        sh *= 2
    o_ref[0, :nh] = gf
    o_ref[0, nh:2 * nh] = gb
    o_ref[0, 2 * nh:] = beta


def _gates(gt, alog, dtb):
    bt, r, sl, _ = gt.shape
    return pl.pallas_call(
        _gates_kernel,
        grid=(bt,),
        in_specs=[
            pl.BlockSpec((1, r, sl, LANE), lambda b: (b, 0, 0, 0)),
            _resident(alog.shape),
            _resident(dtb.shape),
        ],
        out_specs=pl.BlockSpec((1, r, sl, LANE), lambda b: (b, 0, 0, 0)),
        out_shape=jax.ShapeDtypeStruct(gt.shape, F32),
        compiler_params=_params(("parallel",), 32),
        name="gdn_gates",
    )(gt, alog, dtb)


def _unit_tri_inverses(lmats, eye):
    ps = [eye - l for l in lmats]
    ms = [_dotb(l, l) for l in lmats]
    n = 2
    while n < CHUNK // 2:
        ps = [p + _dotb(p, m) for p, m in zip(ps, ms)]
        ms = [_dotb(m, m) for m in ms]
        n *= 2
    return [p + _dotb(p, m) for p, m in zip(ps, ms)]


def _gdn_prep_kernel(q_ref, k_ref, v_ref, grf_ref, grb_ref, gc_ref,
                     wqf_ref, uf_ref, kdf_ref, inf_ref, wqb_ref, ub_ref, kdb_ref, inb_ref):
    ri = lax.broadcasted_iota(I32, (CHUNK, CHUNK), 0)
    ci = lax.broadcasted_iota(I32, (CHUNK, CHUNK), 1)
    eye = (ri == ci).astype(F32)
    dirs = (
        (0, grf_ref, ri >= ci, ri > ci, CHUNK - 1, (wqf_ref, uf_ref, kdf_ref, inf_ref)),
        (1, grb_ref, ri <= ci, ri < ci, 0, (wqb_ref, ub_ref, kdb_ref, inb_ref)),
    )
    lmats = []
    work = []
    for c in range(PREP_CHUNKS):
        rows = slice(c * CHUNK, (c + 1) * CHUNK)
        q = q_ref[0, 0, rows, :]
        k = k_ref[0, 0, rows, :]
        v = v_ref[0, 0, rows, :]
        g4 = gc_ref[0, 0, rows, :]
        kk = _dotb(k, k, NT)
        qk = _dotb(q, k, NT)
        for col, grow_ref, incl, strict, last_row, outs in dirs:
            gcol = g4[:, col:col + 1]
            beta = g4[:, 2 + col:3 + col]
            decay = jnp.exp(jnp.where(incl, gcol - grow_ref[0, 0, c:c + 1, :], -jnp.inf))
            lmats.append(jnp.where(strict, beta * kk * decay, 0.0))
            e_gc = jnp.exp(gcol)
            rhs = jnp.concatenate([v * beta, k * (beta * e_gc)], axis=-1)
            k_dec = k * jnp.exp(g4[last_row:last_row + 1, col:col + 1] - gcol)
            intra = jnp.where(incl, qk * decay, 0.0)
            work.append((c, rows, rhs, q * e_gc, k_dec, intra, outs))
    tinvs = _unit_tri_inverses(lmats, eye)
    for tinv, (c, rows, rhs, q_dec, k_dec, intra, (wq_ref, u_ref, kd_ref, in_ref)) in zip(tinvs, work):
        sol = _dotb(tinv, rhs)
        u_ref[0, 0, rows, :] = sol[:, :GDN_DV]
        wq_ref[0, 0, c, :CHUNK, :] = sol[:, GDN_DV:].astype(BF16)
        wq_ref[0, 0, c, CHUNK:, :] = q_dec.astype(BF16)
        kd_ref[0, 0, rows, :] = k_dec.astype(BF16)
        in_ref[0, 0, rows, :] = intra.astype(BF16)


def _gdn_prep(qkv_h, g_rows, g_cols):
    bt, _, s, _ = qkv_h.shape
    nh = GDN_HEADS
    rb = PREP_CHUNKS * CHUNK
    n_chunks = s // CHUNK
    head = lambda off: (lambda b, h, i: (b, off + h, i, 0))
    seq_spec = lambda w: pl.BlockSpec((1, 1, rb, w), head(0))
    wq_spec = pl.BlockSpec((1, 1, PREP_CHUNKS, 2 * CHUNK, GDN_DK), lambda b, h, i: (b, h, i, 0, 0))
    out_specs = [wq_spec, seq_spec(GDN_DV), seq_spec(GDN_DK), seq_spec(CHUNK)]
    out_shape = [
        jax.ShapeDtypeStruct((bt, nh, n_chunks, 2 * CHUNK, GDN_DK), BF16),
        jax.ShapeDtypeStruct((bt, nh, s, GDN_DV), F32),
        jax.ShapeDtypeStruct((bt, nh, s, GDN_DK), BF16),
        jax.ShapeDtypeStruct((bt, nh, s, CHUNK), BF16),
    ]
    return pl.pallas_call(
        _gdn_prep_kernel,
        grid=(bt, nh, s // rb),
        in_specs=[
            pl.BlockSpec((1, 1, rb, GDN_DK), head(0)),
            pl.BlockSpec((1, 1, rb, GDN_DK), head(nh)),
            pl.BlockSpec((1, 1, rb, GDN_DV), head(2 * nh)),
            pl.BlockSpec((1, 1, PREP_CHUNKS, CHUNK), head(0)),
            pl.BlockSpec((1, 1, PREP_CHUNKS, CHUNK), head(nh)),
            pl.BlockSpec((1, 1, rb, 4), head(0)),
        ],
        out_specs=out_specs + out_specs,
        out_shape=out_shape + out_shape,
        compiler_params=_params(("parallel", "parallel", "parallel"), 48),
        name="gdn_prep",
    )(qkv_h, qkv_h, qkv_h, g_rows, g_rows, g_cols)


def _gdn_scan_kernel(wqf_ref, uf_ref, kdf_ref, inf_ref, grf_ref, wqb_ref, ub_ref, kdb_ref, inb_ref, grb_ref,
                     of_ref, ob_ref, state_ref):
    @pl.when(pl.program_id(1) == 0)
    def _():
        state_ref[...] = jnp.zeros_like(state_ref)

    def step(c, carry):
        dirs = (
            (0, c, wqf_ref, uf_ref, kdf_ref, inf_ref, grf_ref, CHUNK - 1, of_ref),
            (1, PREP_CHUNKS - 1 - c, wqb_ref, ub_ref, kdb_ref, inb_ref, grb_ref, 0, ob_ref),
        )
        chains = [(h,) + d for h in range(GDN_HEADS) for d in dirs]
        states = [state_ref[d, h] for h, d, *_ in chains]
        rs = [_dot(wq_ref[0, h, cc], s.astype(BF16)) for (h, _, cc, wq_ref, *_), s in zip(chains, states)]
        for (h, d, cc, _, u_ref, kd_ref, in_ref, g_ref, last_lane, o_ref), s, r in zip(chains, states, rs):
            rows = pl.ds(pl.multiple_of(cc * CHUNK, CHUNK), CHUNK)
            v_new = (u_ref[0, h, rows, :] - r[:CHUNK]).astype(BF16)
            o_ref[0, rows, h * GDN_DV:(h + 1) * GDN_DV] = r[CHUNK:] + _dot(in_ref[0, h, rows, :], v_new)
            g_last = g_ref[0, d * GDN_HEADS + h, pl.ds(cc, 1), :][:, last_lane:last_lane + 1]
            state_ref[d, h] = s * jnp.exp(g_last) + _dot(kd_ref[0, h, rows, :], v_new, TN)
        return carry

    lax.fori_loop(0, PREP_CHUNKS, step, 0)


def _gdn_scan(prep, g_rows):
    wqf, uf, kdf, inf_, wqb, ub, kdb, inb = prep
    bt, nh, s, _ = uf.shape
    rb = PREP_CHUNKS * CHUNK
    nblk = s // rb
    fwd = lambda b, i: (b, 0, i, 0)
    bwd = lambda b, i: (b, 0, nblk - 1 - i, 0)

    def specs(idx):
        idx5 = lambda b, i: idx(b, i) + (0,)
        return [
            pl.BlockSpec((1, nh, PREP_CHUNKS, 2 * CHUNK, GDN_DK), idx5),
            pl.BlockSpec((1, nh, rb, GDN_DV), idx),
            pl.BlockSpec((1, nh, rb, GDN_DK), idx),
            pl.BlockSpec((1, nh, rb, CHUNK), idx),
            pl.BlockSpec((1, 2 * nh, PREP_CHUNKS, CHUNK), idx),
        ]

    out = jax.ShapeDtypeStruct((bt, s, GDN_WIDTH), F32)
    return pl.pallas_call(
        _gdn_scan_kernel,
        grid=(bt, nblk),
        in_specs=specs(fwd) + specs(bwd),
        out_specs=[
            pl.BlockSpec((1, rb, GDN_WIDTH), lambda b, i: (b, i, 0)),
            pl.BlockSpec((1, rb, GDN_WIDTH), lambda b, i: (b, nblk - 1 - i, 0)),
        ],
        out_shape=[out, out],
        scratch_shapes=[pltpu.VMEM((2, nh, GDN_DK, GDN_DV), F32)],
        compiler_params=_params(("parallel", "arbitrary"), 48),
        name="gdn_scan",
    )(wqf, uf, kdf, inf_, g_rows, wqb, ub, kdb, inb, g_rows)


def _out_proj_kernel(x_ref, mla_ref, of_ref, ob_ref, z_ref, gn_ref, wo_ref, g_ref, wr_ref, x1_ref, xn_ref, aff_ref,
                     gdn_ref):
    for h in range(GDN_HEADS):
        cols = slice(h * GDN_DV, (h + 1) * GDN_DV)
        o = of_ref[:, cols] + ob_ref[:, cols]
        on = o * lax.rsqrt(jnp.mean(o * o, axis=-1, keepdims=True) + EPS) * gn_ref[...]
        gdn_ref[:, cols] = (on * _silu(z_ref[:, cols].astype(F32))).astype(BF16)
    y = x_ref[...] + _dot(mla_ref[...], wo_ref[:MLA_WIDTH, :]) + _dot(gdn_ref[...], wo_ref[MLA_WIDTH:, :])
    x1_ref[...] = y
    xn = y * lax.rsqrt(jnp.mean(y * y, axis=-1, keepdims=True) + EPS) * g_ref[...]
    xn_ref[...] = xn
    logits = _dot3(xn, wr_ref[...])
    e = jnp.exp(logits - jnp.max(logits, axis=-1, keepdims=True))
    aff_ref[...] = e / jnp.sum(e, axis=-1, keepdims=True)


def _out_proj(x2d, mla, o_fwd, o_bwd, oa, gn, wo, g2, wr):
    t = x2d.shape[0]
    tm = min(256, t)
    rows = lambda w: pl.BlockSpec((tm, w), lambda i: (i, 0))
    return pl.pallas_call(
        _out_proj_kernel,
        grid=(t // tm,),
        in_specs=[rows(D_MODEL), rows(MLA_WIDTH), rows(GDN_WIDTH), rows(GDN_WIDTH), rows(GDN_WIDTH),
                  _resident(gn.shape), _resident(wo.shape), _resident(g2.shape), _resident(wr.shape)],
        out_specs=[rows(D_MODEL), rows(D_MODEL), rows(N_EXPERTS)],
        out_shape=[
            jax.ShapeDtypeStruct((t, D_MODEL), F32),
            jax.ShapeDtypeStruct((t, D_MODEL), F32),
            jax.ShapeDtypeStruct((t, N_EXPERTS), F32),
        ],
        scratch_shapes=[pltpu.VMEM((tm, GDN_WIDTH), BF16)],
        compiler_params=_params(("parallel",), 48),
        name="out_proj",
    )(x2d, mla, o_fwd, o_bwd, oa, gn, wo, g2, wr)


def _ffn_kernel(idx_ref, xn_hbm, gate_ref, wg_ref, wu_ref, wd_ref, ye_ref, stage_ref, xe_ref, acc_ref, sem):
    e = pl.program_id(0)
    f = pl.program_id(1)
    cap = xe_ref.shape[0]

    def start_gather(expert, buf):
        def start(slot, c):
            tok = idx_ref[expert * cap + slot]
            pltpu.make_async_copy(xn_hbm.at[pl.ds(tok, 1), :], stage_ref.at[buf, pl.ds(slot, 1), :],
                                  sem.at[buf]).start()
            return c

        lax.fori_loop(0, cap, start, 0, unroll=8)

    @pl.when(f == 0)
    def _():
        @pl.when(e == 0)
        def _():
            start_gather(0, 0)

        buf = e % 2
        pltpu.make_async_copy(xn_hbm.at[pl.ds(0, cap), :], stage_ref.at[buf], sem.at[buf]).wait()
        xe_ref[...] = stage_ref[buf].astype(BF16)
        acc_ref[...] = jnp.zeros_like(acc_ref)

        @pl.when(e + 1 < pl.num_programs(0))
        def _():
            start_gather(e + 1, 1 - buf)

    xe = xe_ref[...]
    g = _dot(xe, wg_ref[0].astype(BF16))
    u = _dot(xe, wu_ref[0].astype(BF16))
    h = (_silu(g) * u).astype(BF16)
    acc_ref[...] += _dot(h, wd_ref[0].astype(BF16))

    @pl.when(f == pl.num_programs(1) - 1)
    def _():
        ye_ref[0] = (acc_ref[...] * gate_ref[0]).astype(BF16)


def _ffn(idx, xn, gates3, w_gate, w_up, w_down):
    e, cap = idx.shape
    tf = 256
    grid_spec = pltpu.PrefetchScalarGridSpec(
        num_scalar_prefetch=1,
        grid=(e, EXPERT_FF // tf),
        in_specs=[
            pl.BlockSpec(memory_space=pl.ANY),
            pl.BlockSpec((1, cap, 1), lambda ee, f, ix: (ee, 0, 0)),
            pl.BlockSpec((1, D_MODEL, tf), lambda ee, f, ix: (ee, 0, f)),
            pl.BlockSpec((1, D_MODEL, tf), lambda ee, f, ix: (ee, 0, f)),
            pl.BlockSpec((1, tf, D_MODEL), lambda ee, f, ix: (ee, f, 0)),
        ],
        out_specs=pl.BlockSpec((1, cap, D_MODEL), lambda ee, f, ix: (ee, 0, 0)),
        scratch_shapes=[
            pltpu.VMEM((2, cap, D_MODEL), F32),
            pltpu.VMEM((cap, D_MODEL), BF16),
            pltpu.VMEM((cap, D_MODEL), F32),
            pltpu.SemaphoreType.DMA((2,)),
        ],
    )
    return pl.pallas_call(
        _ffn_kernel,
        grid_spec=grid_spec,
        out_shape=jax.ShapeDtypeStruct((e, cap, D_MODEL), BF16),
        compiler_params=_params(("arbitrary", "arbitrary"), 56),
        name="moe_ffn",
    )(idx.reshape(-1), xn, gates3, w_gate, w_up, w_down)


def _combine_kernel(lo_ref, slot_ref, x1_ref, ye_hbm, y_ref, win_ref, sem):
    i = pl.program_id(0)
    nt = pl.num_programs(0)
    n_e = win_ref.shape[1]
    width = win_ref.shape[2]

    def window_copy(tile, buf, e):
        lo = pl.multiple_of(lo_ref[tile * n_e + e], BF16_ROWS)
        return pltpu.make_async_copy(ye_hbm.at[e, pl.ds(lo, width), :], win_ref.at[buf, e], sem.at[buf, e])

    @pl.when(i == 0)
    def _():
        for e in range(n_e):
            window_copy(0, 0, e).start()

    @pl.when(i + 1 < nt)
    def _():
        for e in range(n_e):
            window_copy(i + 1, (i + 1) % 2, e).start()

    buf = i % 2
    slot = slot_ref[...]
    acc = x1_ref[...]
    col = lax.broadcasted_iota(I32, (slot.shape[0], width), 1)
    for e in range(n_e):
        window_copy(i, buf, e).wait()
        rel = slot[:, e:e + 1] - lo_ref[i * n_e + e]
        onehot = (rel == col).astype(BF16)
        acc = acc + _dot(onehot, win_ref[buf, e])
    y_ref[...] = acc


def _combine(lo_flat, slot, x1, ye, tt):
    t = x1.shape[0]
    n_e = ye.shape[0]
    width = tt + BF16_ROWS
    grid_spec = pltpu.PrefetchScalarGridSpec(
        num_scalar_prefetch=1,
        grid=(t // tt,),
        in_specs=[
            pl.BlockSpec((tt, n_e), lambda i, lo: (i, 0)),
            pl.BlockSpec((tt, D_MODEL), lambda i, lo: (i, 0)),
            pl.BlockSpec(memory_space=pl.ANY),
        ],
        out_specs=pl.BlockSpec((tt, D_MODEL), lambda i, lo: (i, 0)),
        scratch_shapes=[
            pltpu.VMEM((2, n_e, width, D_MODEL), BF16),
            pltpu.SemaphoreType.DMA((2, n_e)),
        ],
    )
    return pl.pallas_call(
        _combine_kernel,
        grid_spec=grid_spec,
        out_shape=jax.ShapeDtypeStruct((t, D_MODEL), F32),
        compiler_params=_params(("arbitrary",), 48),
        name="moe_combine",
    )(lo_flat, slot, x1, ye)


def _rope_tables(s):
    half = QK_ROPE // 2
    inv_freq = ROPE_THETA ** (-jnp.arange(half, dtype=F32) / half)
    ang = jnp.arange(s, dtype=F32)[:, None] * inv_freq[None, :]
    cos = jnp.cos(ang)
    sin = jnp.sin(ang)
    zeros = jnp.zeros((s, LANE - QK_ROPE), F32)
    return jnp.concatenate([cos, cos, zeros], axis=1), jnp.concatenate([-sin, sin, zeros], axis=1)


def _prepare_weights(norm1_g, w_in, q_a_norm_g, w_q_b, kv_a_norm_g, w_kv_b, q_norm_g, k_norm_g, conv_w,
                     a_log_fwd, a_log_bwd, dt_bias_fwd, dt_bias_bwd, gdn_norm_g, w_o, norm2_g, w_router):
    o_kv = Q_LORA
    o_kr = o_kv + KV_LORA
    o_qkv = o_kr + QK_ROPE
    o_z = o_qkv + QKV_GDN
    o_g = o_z + GDN_WIDTH
    wa = jnp.concatenate([w_in[:, o_z:o_g], w_in[:, :o_kr]], axis=1).astype(BF16)
    wb = jnp.concatenate(
        [w_in[:, o_qkv:o_z], w_in[:, o_kr:o_qkv], w_in[:, o_g:], jnp.zeros((D_MODEL, LANE - QK_ROPE - 4 * GDN_HEADS), F32)],
        axis=1).astype(BF16)
    wq = jnp.pad(w_q_b.reshape(Q_LORA, MLA_HEADS, QK_HEAD), ((0, 0), (0, 0), (0, QK_PAD - QK_HEAD)))
    wq = wq.reshape(Q_LORA, MLA_HEADS * QK_PAD).astype(BF16)
    pad_g = lambda g: jnp.pad(g, (0, QK_PAD - QK_HEAD)).reshape(1, QK_PAD)
    alog = jnp.concatenate([a_log_fwd, a_log_bwd]).reshape(2 * GDN_HEADS, 1, 1)
    dtb = jnp.concatenate([dt_bias_fwd, dt_bias_bwd]).reshape(2 * GDN_HEADS, 1, 1)
    return dict(
        g1=norm1_g.reshape(1, D_MODEL), wa=wa, wb=wb,
        gqa=q_a_norm_g.reshape(1, Q_LORA), gkva=kv_a_norm_g.reshape(1, KV_LORA),
        wq=wq, wkv=w_kv_b.astype(BF16), gq=pad_g(q_norm_g), gk=pad_g(k_norm_g),
        conv_w=conv_w,
        alog=jnp.broadcast_to(alog, (2 * GDN_HEADS, 1, LANE)), dtb=jnp.broadcast_to(dtb, (2 * GDN_HEADS, 1, LANE)),
        gn=gdn_norm_g.reshape(1, GDN_DV), wo=w_o.astype(BF16), g2=norm2_g.reshape(1, D_MODEL), wr=w_router,
    )


def _route(aff, cap, tt):
    t, n_e = aff.shape
    gates, idx = lax.top_k(aff.T, cap)
    order = jnp.argsort(idx, axis=1)
    idx = jnp.take_along_axis(idx, order, axis=1)
    gates = jnp.take_along_axis(gates, order, axis=1)
    e_ids = jnp.broadcast_to(jnp.arange(n_e, dtype=I32)[:, None], idx.shape)
    s_ids = jnp.broadcast_to(jnp.arange(cap, dtype=I32)[None, :], idx.shape)
    slot = jnp.full((t, n_e), -1, I32).at[idx, e_ids].set(s_ids)
    per_tile = (slot >= 0).reshape(t // tt, tt, n_e).sum(axis=1).astype(I32)
    lo = jnp.cumsum(per_tile, axis=0) - per_tile
    lo = jnp.minimum((lo // BF16_ROWS) * BF16_ROWS, cap - (tt + BF16_ROWS))
    return idx.astype(I32), gates, slot, lo.reshape(-1)


def _encoder_layer(x, p, w_gate, w_up, w_down):
    bt, s, _ = x.shape
    t = bt * s
    x2d = x.reshape(t, D_MODEL)
    oa, ob = _in_proj(x2d, p["g1"], p["wa"], p["wb"])

    cos, sin = _rope_tables(s)
    q, k, v = _mla_prep(oa, ob, cos, sin, p["gqa"], p["gkva"], p["wq"], p["wkv"], p["gq"], p["gk"], bt, s)
    mla_out = _attention(q, k, v)

    ob3 = ob.reshape(bt, s, NB)
    qkv_h = _conv(ob3, p["conv_w"])
    g_raw = ob3[:, :, QKV_GDN + QK_ROPE:QKV_GDN + QK_ROPE + 4 * GDN_HEADS]
    g_t = jnp.swapaxes(g_raw, 1, 2).reshape(bt, 4 * GDN_HEADS, s // LANE, LANE)
    g_out = _gates(g_t, p["alog"], p["dtb"]).reshape(bt, 4, GDN_HEADS, s)
    g_rows = g_out[:, :2].reshape(bt, 2 * GDN_HEADS, s // CHUNK, CHUNK)
    g_cols = jnp.transpose(g_out, (0, 2, 3, 1))
    o_fwd, o_bwd = _gdn_scan(_gdn_prep(qkv_h, g_rows, g_cols), g_rows)

    x1, xn, aff = _out_proj(x2d, mla_out.reshape(t, MLA_WIDTH), o_fwd.reshape(t, GDN_WIDTH),
                            o_bwd.reshape(t, GDN_WIDTH), oa, p["gn"], p["wo"], p["g2"], p["wr"])

    cap = max(1, CAPACITY_FACTOR * t // N_EXPERTS)
    tt = 128
    idx, gates, slot, lo = _route(aff, cap, tt)
    ye = _ffn(idx, xn, gates.reshape(N_EXPERTS, cap, 1), w_gate, w_up, w_down)
    y = _combine(lo, slot, x1, ye, tt)
    return y.reshape(bt, s, D_MODEL)


def kernel(x_prompt, x_sample, norm1_g, w_in, q_a_norm_g, w_q_b, kv_a_norm_g, w_kv_b, q_norm_g, k_norm_g, conv_w,
           a_log_fwd, a_log_bwd, dt_bias_fwd, dt_bias_bwd, gdn_norm_g, w_o, norm2_g, w_router, w_gate, w_up, w_down):
    y_prompt = x_prompt
    y_sample = x_sample
    for l in range(norm1_g.shape[0]):
        p = _prepare_weights(norm1_g[l], w_in[l], q_a_norm_g[l], w_q_b[l], kv_a_norm_g[l], w_kv_b[l], q_norm_g[l],
                             k_norm_g[l], conv_w[l], a_log_fwd[l], a_log_bwd[l], dt_bias_fwd[l], dt_bias_bwd[l],
                             gdn_norm_g[l], w_o[l], norm2_g[l], w_router[l])
        y_prompt = _encoder_layer(y_prompt, p, w_gate[l], w_up[l], w_down[l])
        y_sample = _encoder_layer(y_sample, p, w_gate[l], w_up[l], w_down[l])
    return (y_prompt, y_sample)
```

```python
import functools
import math

import jax
import jax.numpy as jnp
from jax import lax
from jax.experimental import pallas as pl
from jax.experimental.pallas import tpu as pltpu

F32 = jnp.float32
BF16 = jnp.bfloat16
I32 = jnp.int32

D_MODEL = 2048
MLA_HEADS = 8
Q_LORA = 512
KV_LORA = 256
QK_NOPE = 128
QK_ROPE = 64
QK_HEAD = QK_NOPE + QK_ROPE
QK_PAD = 256
V_HEAD = 128
V_PAD = 256
ROPE_THETA = 10000.0
GDN_HEADS = 8
GDN_DK = 128
GDN_DV = 128
QKV_GDN = 2 * GDN_HEADS * GDN_DK + GDN_HEADS * GDN_DV
GDN_WIDTH = GDN_HEADS * GDN_DV
MLA_WIDTH = MLA_HEADS * V_HEAD
CONV_W = 5
CHUNK = 64
N_EXPERTS = 16
EXPERT_FF = 1024
CAPACITY_FACTOR = 2
EPS = 1e-6

LANE = 128
SUBLANES = 8
BF16_ROWS = 16
MIB = 1024 * 1024

NA = GDN_WIDTH + Q_LORA + KV_LORA
NB = QKV_GDN + LANE
QLAT_COL_BLOCK = GDN_WIDTH // Q_LORA
KVLAT_COL_BLOCK = (GDN_WIDTH + Q_LORA) // KV_LORA
MISC_COL_BLOCK = QKV_GDN // LANE
PREP_CHUNKS = 8
ROUTE_BLOCK = LANE
COMBINE_K = 256
EXP_SPAN = 126
MANTISSA_STEPS = 40

NN = (((1,), (0,)), ((), ()))
NT = (((1,), (1,)), ((), ()))
TN = (((0,), (0,)), ((), ()))


def _params(semantics, vmem_mib):
    return pltpu.CompilerParams(dimension_semantics=semantics, vmem_limit_bytes=vmem_mib * MIB)


def _resident(shape):
    nd = len(shape)
    return pl.BlockSpec(shape, lambda *_: (0,) * nd, pipeline_mode=pl.Buffered(1))


def _dot(a, b, dims=NN):
    return lax.dot_general(a, b, dims, preferred_element_type=F32)


def _dotb(a, b, dims=NN):
    return lax.dot_general(a.astype(BF16), b.astype(BF16), dims, preferred_element_type=F32)


def _split(x):
    hi = x.astype(BF16)
    lo = (x - hi.astype(F32)).astype(BF16)
    return hi, lo


def _dot3(a, b, dims=NN):
    ah, al = _split(a)
    bh, bl = _split(b)
    return _dot(ah, bh, dims) + (_dot(ah, bl, dims) + _dot(al, bh, dims))


def _silu(x):
    return x * jax.nn.sigmoid(x)


def _in_proj_kernel(x_ref, g_ref, wa_ref, wb_ref, oa_ref, ob_ref):
    x = x_ref[...]
    ms = jnp.mean(x * x, axis=-1, keepdims=True)
    h = (x * lax.rsqrt(ms + EPS) * g_ref[...]).astype(BF16)
    oa_ref[...] = _dot(h, wa_ref[...]).astype(BF16)
    ob_ref[...] = _dot(h, wb_ref[...])


def _in_proj(x2d, g, wa, wb):
    t = x2d.shape[0]
    tm = min(512, t)
    return pl.pallas_call(
        _in_proj_kernel,
        grid=(t // tm,),
        in_specs=[
            pl.BlockSpec((tm, D_MODEL), lambda i: (i, 0)),
            _resident((1, D_MODEL)),
            _resident(wa.shape),
            _resident(wb.shape),
        ],
        out_specs=[
            pl.BlockSpec((tm, NA), lambda i: (i, 0)),
            pl.BlockSpec((tm, NB), lambda i: (i, 0)),
        ],
        out_shape=[
            jax.ShapeDtypeStruct((t, NA), BF16),
            jax.ShapeDtypeStruct((t, NB), F32),
        ],
        compiler_params=_params(("parallel",), 56),
        name="in_proj",
    )(x2d, g, wa, wb)


def _mla_prep_kernel(ql_ref, kvl_ref, misc_ref, cos_ref, sin_ref, gqa_ref, gkva_ref, wq_ref, wkv_ref,
                     gq_ref, gk_ref, q_ref, k_ref, v_ref):
    def norm(x, g):
        return x * lax.rsqrt(jnp.mean(x * x, axis=-1, keepdims=True) + EPS) * g

    qn = norm(ql_ref[...].astype(F32), gqa_ref[...]).astype(BF16)
    kvn = norm(kvl_ref[...].astype(F32), gkva_ref[...]).astype(BF16)
    q = _dot(qn, wq_ref[...])
    kv = _dot(kvn, wkv_ref[...])

    cos = cos_ref[...]
    sin = sin_ref[...]
    lane = lax.broadcasted_iota(I32, cos.shape, 1)

    def rope(t):
        rot = jnp.where(lane < QK_ROPE // 2, pltpu.roll(t, LANE - QK_ROPE // 2, 1), pltpu.roll(t, QK_ROPE // 2, 1))
        return t * cos + rot * sin

    gq = gq_ref[...]
    gk = gk_ref[...]
    kr = jnp.where(lane < QK_ROPE, misc_ref[...], 0.0)
    kr_ssq = jnp.sum(kr * kr, axis=-1, keepdims=True)
    kr_rot = rope(kr * gk[:, QK_NOPE:])
    q_scale = QK_HEAD ** -0.5
    ones_col = jnp.where(lane == 0, 1.0, 0.0).astype(BF16)
    for h in range(MLA_HEADS):
        qh = q[:, h * QK_PAD:(h + 1) * QK_PAD]
        sc = lax.rsqrt(jnp.sum(qh * qh, axis=-1, keepdims=True) * (1.0 / QK_HEAD) + EPS) * q_scale
        q_ref[0, h, :, :QK_NOPE] = (qh[:, :QK_NOPE] * gq[:, :QK_NOPE] * sc).astype(BF16)
        q_ref[0, h, :, QK_NOPE:] = (rope(qh[:, QK_NOPE:] * gq[:, QK_NOPE:]) * sc).astype(BF16)
        kn = kv[:, h * 256:h * 256 + QK_NOPE]
        sk = lax.rsqrt((jnp.sum(kn * kn, axis=-1, keepdims=True) + kr_ssq) * (1.0 / QK_HEAD) + EPS)
        k_ref[0, h, :, :QK_NOPE] = (kn * gk[:, :QK_NOPE] * sk).astype(BF16)
        k_ref[0, h, :, QK_NOPE:] = (kr_rot * sk).astype(BF16)
        v_ref[0, h, :, :V_HEAD] = kv[:, h * 256 + QK_NOPE:(h + 1) * 256].astype(BF16)
        v_ref[0, h, :, V_HEAD:] = ones_col


def _mla_prep(oa, ob, cos, sin, gqa, gkva, wq, wkv, gq, gk, bt, s):
    tm = min(512, s)
    nt = s // tm
    return pl.pallas_call(
        _mla_prep_kernel,
        grid=(bt, nt),
        in_specs=[
            pl.BlockSpec((tm, Q_LORA), lambda b, i: (b * nt + i, QLAT_COL_BLOCK)),
            pl.BlockSpec((tm, KV_LORA), lambda b, i: (b * nt + i, KVLAT_COL_BLOCK)),
            pl.BlockSpec((tm, LANE), lambda b, i: (b * nt + i, MISC_COL_BLOCK)),
            pl.BlockSpec((tm, LANE), lambda b, i: (i, 0)),
            pl.BlockSpec((tm, LANE), lambda b, i: (i, 0)),
            _resident(gqa.shape),
            _resident(gkva.shape),
            _resident(wq.shape),
            _resident(wkv.shape),
            _resident(gq.shape),
            _resident(gk.shape),
        ],
        out_specs=[
            pl.BlockSpec((1, MLA_HEADS, tm, QK_PAD), lambda b, i: (b, 0, i, 0)),
            pl.BlockSpec((1, MLA_HEADS, tm, QK_PAD), lambda b, i: (b, 0, i, 0)),
            pl.BlockSpec((1, MLA_HEADS, tm, V_PAD), lambda b, i: (b, 0, i, 0)),
        ],
        out_shape=[
            jax.ShapeDtypeStruct((bt, MLA_HEADS, s, QK_PAD), BF16),
            jax.ShapeDtypeStruct((bt, MLA_HEADS, s, QK_PAD), BF16),
            jax.ShapeDtypeStruct((bt, MLA_HEADS, s, V_PAD), BF16),
        ],
        compiler_params=_params(("parallel", "parallel"), 48),
        name="mla_prep",
    )(oa, oa, ob, cos, sin, gqa, gkva, wq, wkv, gq, gk)


def _attn_kernel(q_ref, k_ref, v_ref, o_ref):
    k = k_ref[0, 0]
    v = v_ref[0, 0]
    half = q_ref.shape[2] // 2
    halves = (slice(0, half), slice(half, 2 * half))
    scores = [_dot(q_ref[0, 0, rows, :], k, NT) for rows in halves]
    for rows, s in zip(halves, scores):
        m = jnp.max(s, axis=-1, keepdims=True)
        p = jnp.exp((s - m).astype(BF16))
        o = _dot(p, v)
        o_ref[0, rows, :] = (o[:, :V_HEAD] / o[:, V_HEAD:V_HEAD + 1]).astype(BF16)


def _attention(q, k, v):
    bt, h, s, _ = q.shape
    tq = min(512, s)
    return pl.pallas_call(
        _attn_kernel,
        grid=(bt, h, s // tq),
        in_specs=[
            pl.BlockSpec((1, 1, tq, QK_PAD), lambda b, hh, i: (b, hh, i, 0)),
            pl.BlockSpec((1, 1, s, QK_PAD), lambda b, hh, i: (b, hh, 0, 0)),
            pl.BlockSpec((1, 1, s, V_PAD), lambda b, hh, i: (b, hh, 0, 0)),
        ],
        out_specs=pl.BlockSpec((1, tq, V_HEAD), lambda b, hh, i: (b, i, hh)),
        out_shape=jax.ShapeDtypeStruct((bt, s, MLA_WIDTH), BF16),
        compiler_params=_params(("parallel", "parallel", "parallel"), 48),
        name="attention",
    )(q, k, v)


def _conv_kernel(x_ref, w_ref, o_ref):
    c = pl.program_id(1)
    x = x_ref[0]
    s = x.shape[0]
    w = w_ref[...]
    pad = CONV_W // 2
    taps = {d: w[pad + d:pad + d + 1] for d in range(-pad, pad + 1)}
    shifted = {d: (x if d == 0 else pltpu.roll(x, (-d) % s, 0)) for d in taps}
    is_q = c < GDN_HEADS
    is_qk = c < 2 * GDN_HEADS

    def finish(acc):
        y = _silu(acc)
        inv = lax.rsqrt(jnp.sum(y * y, axis=-1, keepdims=True) + EPS)
        return y * (jnp.where(is_qk, inv, 1.0) * jnp.where(is_q, GDN_DK ** -0.5, 1.0))

    o_ref[0, 0] = finish(sum(shifted[d] * taps[d] for d in taps))
    edge_row = lax.broadcasted_iota(I32, (SUBLANES, LANE), 0)
    for first in (0, s - SUBLANES):
        rows = slice(first, first + SUBLANES)
        acc = jnp.zeros((SUBLANES, LANE), F32)
        for d in taps:
            src = edge_row + (first + d)
            acc = acc + jnp.where(jnp.logical_and(src >= 0, src < s), shifted[d][rows], 0.0) * taps[d]
        o_ref[0, 0, rows, :] = finish(acc)


def _conv(ob3, conv_w):
    bt, s, _ = ob3.shape
    nblk = QKV_GDN // LANE
    return pl.pallas_call(
        _conv_kernel,
        grid=(bt, nblk),
        in_specs=[
            pl.BlockSpec((1, s, LANE), lambda b, c: (b, 0, c)),
            pl.BlockSpec((CONV_W, LANE), lambda b, c: (0, c)),
        ],
        out_specs=pl.BlockSpec((1, 1, s, LANE), lambda b, c: (b, c, 0, 0)),
        out_shape=jax.ShapeDtypeStruct((bt, nblk, s, LANE), F32),
        compiler_params=_params(("parallel", "parallel"), 48),
        name="gdn_conv",
    )(ob3, conv_w)


def _gates_kernel(x_ref, alog_ref, dtb_ref, o_ref):
    x = x_ref[0]
    nh = GDN_HEADS
    a = x[:2 * nh] + dtb_ref[...]
    softplus = jnp.maximum(a, 0.0) + jnp.log1p(jnp.exp(-jnp.abs(a)))
    g = -jnp.exp(alog_ref[...]) * softplus
    beta = jax.nn.sigmoid(x[2 * nh:])
    gf = g[:nh]
    gb = g[nh:]
    pos = lax.broadcasted_iota(I32, gf.shape, 2) % CHUNK
    sh = 1
    while sh < CHUNK:
        gf = gf + jnp.where(pos >= sh, pltpu.roll(gf, sh, 2), 0.0)
        gb = gb + jnp.where(pos < CHUNK - sh, pltpu.roll(gb, LANE - sh, 2), 0.0)
        sh *= 2
    o_ref[0, :nh] = gf
    o_ref[0, nh:2 * nh] = gb
    o_ref[0, 2 * nh:] = beta


def _gates(gt, alog, dtb):
    bt, r, sl, _ = gt.shape
    return pl.pallas_call(
        _gates_kernel,
        grid=(bt,),
        in_specs=[
            pl.BlockSpec((1, r, sl, LANE), lambda b: (b, 0, 0, 0)),
            _resident(alog.shape),
            _resident(dtb.shape),
        ],
        out_specs=pl.BlockSpec((1, r, sl, LANE), lambda b: (b, 0, 0, 0)),
        out_shape=jax.ShapeDtypeStruct(gt.shape, F32),
        compiler_params=_params(("parallel",), 32),
        name="gdn_gates",
    )(gt, alog, dtb)


def _unit_tri_inverses(lmats, eye):
    ps = [eye - l for l in lmats]
    ms = [_dotb(l, l) for l in lmats]
    n = 2
    while n < CHUNK // 2:
        ps = [p + _dotb(p, m) for p, m in zip(ps, ms)]
        ms = [_dotb(m, m) for m in ms]
        n *= 2
    return [p + _dotb(p, m) for p, m in zip(ps, ms)]


def _gdn_prep_kernel(q_ref, k_ref, v_ref, grf_ref, grb_ref, gc_ref,
                     wqf_ref, uf_ref, kdf_ref, inf_ref, wqb_ref, ub_ref, kdb_ref, inb_ref):
    ri = lax.broadcasted_iota(I32, (CHUNK, CHUNK), 0)
    ci = lax.broadcasted_iota(I32, (CHUNK, CHUNK), 1)
    eye = (ri == ci).astype(F32)
    dirs = (
        (0, grf_ref, ri >= ci, ri > ci, CHUNK - 1, (wqf_ref, uf_ref, kdf_ref, inf_ref)),
        (1, grb_ref, ri <= ci, ri < ci, 0, (wqb_ref, ub_ref, kdb_ref, inb_ref)),
    )
    lmats = []
    work = []
    for c in range(PREP_CHUNKS):
        rows = slice(c * CHUNK, (c + 1) * CHUNK)
        q = q_ref[0, 0, rows, :]
        k = k_ref[0, 0, rows, :]
        v = v_ref[0, 0, rows, :]
        g4 = gc_ref[0, 0, rows, :]
        kk = _dotb(k, k, NT)
        qk = _dotb(q, k, NT)
        for col, grow_ref, incl, strict, last_row, outs in dirs:
            gcol = g4[:, col:col + 1]
            beta = g4[:, 2 + col:3 + col]
            decay = jnp.exp(jnp.where(incl, gcol - grow_ref[0, 0, c:c + 1, :], -jnp.inf))
            lmats.append(jnp.where(strict, beta * kk * decay, 0.0))
            e_gc = jnp.exp(gcol)
            rhs = jnp.concatenate([v * beta, k * (beta * e_gc)], axis=-1)
            k_dec = k * jnp.exp(g4[last_row:last_row + 1, col:col + 1] - gcol)
            intra = jnp.where(incl, qk * decay, 0.0)
            work.append((c, rows, rhs, q * e_gc, k_dec, intra, outs))
    tinvs = _unit_tri_inverses(lmats, eye)
    for tinv, (c, rows, rhs, q_dec, k_dec, intra, (wq_ref, u_ref, kd_ref, in_ref)) in zip(tinvs, work):
        sol = _dotb(tinv, rhs)
        u_ref[0, 0, rows, :] = sol[:, :GDN_DV]
        wq_ref[0, 0, c, :CHUNK, :] = sol[:, GDN_DV:].astype(BF16)
        wq_ref[0, 0, c, CHUNK:, :] = q_dec.astype(BF16)
        kd_ref[0, 0, rows, :] = k_dec.astype(BF16)
        in_ref[0, 0, rows, :] = intra.astype(BF16)


def _gdn_prep(qkv_h, g_rows, g_cols):
    bt, _, s, _ = qkv_h.shape
    nh = GDN_HEADS
    rb = PREP_CHUNKS * CHUNK
    n_chunks = s // CHUNK
    head = lambda off: (lambda b, h, i: (b, off + h, i, 0))
    seq_spec = lambda w: pl.BlockSpec((1, 1, rb, w), head(0))
    wq_spec = pl.BlockSpec((1, 1, PREP_CHUNKS, 2 * CHUNK, GDN_DK), lambda b, h, i: (b, h, i, 0, 0))
    out_specs = [wq_spec, seq_spec(GDN_DV), seq_spec(GDN_DK), seq_spec(CHUNK)]
    out_shape = [
        jax.ShapeDtypeStruct((bt, nh, n_chunks, 2 * CHUNK, GDN_DK), BF16),
        jax.ShapeDtypeStruct((bt, nh, s, GDN_DV), F32),
        jax.ShapeDtypeStruct((bt, nh, s, GDN_DK), BF16),
        jax.ShapeDtypeStruct((bt, nh, s, CHUNK), BF16),
    ]
    return pl.pallas_call(
        _gdn_prep_kernel,
        grid=(bt, nh, s // rb),
        in_specs=[
            pl.BlockSpec((1, 1, rb, GDN_DK), head(0)),
            pl.BlockSpec((1, 1, rb, GDN_DK), head(nh)),
            pl.BlockSpec((1, 1, rb, GDN_DV), head(2 * nh)),
            pl.BlockSpec((1, 1, PREP_CHUNKS, CHUNK), head(0)),
            pl.BlockSpec((1, 1, PREP_CHUNKS, CHUNK), head(nh)),
            pl.BlockSpec((1, 1, rb, 4), head(0)),
        ],
        out_specs=out_specs + out_specs,
        out_shape=out_shape + out_shape,
        compiler_params=_params(("parallel", "parallel", "parallel"), 48),
        name="gdn_prep",
    )(qkv_h, qkv_h, qkv_h, g_rows, g_rows, g_cols)


def _gdn_scan_kernel(wqf_ref, uf_ref, kdf_ref, inf_ref, grf_ref, wqb_ref, ub_ref, kdb_ref, inb_ref, grb_ref,
                     of_ref, ob_ref, state_ref):
    @pl.when(pl.program_id(1) == 0)
    def _():
        state_ref[...] = jnp.zeros_like(state_ref)

    def step(c, carry):
        dirs = (
            (0, c, wqf_ref, uf_ref, kdf_ref, inf_ref, grf_ref, CHUNK - 1, of_ref),
            (1, PREP_CHUNKS - 1 - c, wqb_ref, ub_ref, kdb_ref, inb_ref, grb_ref, 0, ob_ref),
        )
        chains = [(h,) + d for h in range(GDN_HEADS) for d in dirs]
        states = [state_ref[d, h] for h, d, *_ in chains]
        rs = [_dot(wq_ref[0, h, cc], s.astype(BF16)) for (h, _, cc, wq_ref, *_), s in zip(chains, states)]
        for (h, d, cc, _, u_ref, kd_ref, in_ref, g_ref, last_lane, o_ref), s, r in zip(chains, states, rs):
            rows = pl.ds(pl.multiple_of(cc * CHUNK, CHUNK), CHUNK)
            v_new = (u_ref[0, h, rows, :] - r[:CHUNK]).astype(BF16)
            o_ref[0, rows, h * GDN_DV:(h + 1) * GDN_DV] = r[CHUNK:] + _dot(in_ref[0, h, rows, :], v_new)
            g_last = g_ref[0, d * GDN_HEADS + h, pl.ds(cc, 1), :][:, last_lane:last_lane + 1]
            state_ref[d, h] = s * jnp.exp(g_last) + _dot(kd_ref[0, h, rows, :], v_new, TN)
        return carry

    lax.fori_loop(0, PREP_CHUNKS, step, 0)


def _gdn_scan(prep, g_rows):
    wqf, uf, kdf, inf_, wqb, ub, kdb, inb = prep
    bt, nh, s, _ = uf.shape
    rb = PREP_CHUNKS * CHUNK
    nblk = s // rb
    fwd = lambda b, i: (b, 0, i, 0)
    bwd = lambda b, i: (b, 0, nblk - 1 - i, 0)

    def specs(idx):
        idx5 = lambda b, i: idx(b, i) + (0,)
        return [
            pl.BlockSpec((1, nh, PREP_CHUNKS, 2 * CHUNK, GDN_DK), idx5),
            pl.BlockSpec((1, nh, rb, GDN_DV), idx),
            pl.BlockSpec((1, nh, rb, GDN_DK), idx),
            pl.BlockSpec((1, nh, rb, CHUNK), idx),
            pl.BlockSpec((1, 2 * nh, PREP_CHUNKS, CHUNK), idx),
        ]

    out = jax.ShapeDtypeStruct((bt, s, GDN_WIDTH), F32)
    return pl.pallas_call(
        _gdn_scan_kernel,
        grid=(bt, nblk),
        in_specs=specs(fwd) + specs(bwd),
        out_specs=[
            pl.BlockSpec((1, rb, GDN_WIDTH), lambda b, i: (b, i, 0)),
            pl.BlockSpec((1, rb, GDN_WIDTH), lambda b, i: (b, nblk - 1 - i, 0)),
        ],
        out_shape=[out, out],
        scratch_shapes=[pltpu.VMEM((2, nh, GDN_DK, GDN_DV), F32)],
        compiler_params=_params(("parallel", "arbitrary"), 48),
        name="gdn_scan",
    )(wqf, uf, kdf, inf_, g_rows, wqb, ub, kdb, inb, g_rows)


def _out_proj_kernel(x_ref, mla_ref, of_ref, ob_ref, z_ref, gn_ref, wo_ref, g_ref, wr_ref, x1_ref, xn_ref, aff_ref,
                     gdn_ref):
    for h in range(GDN_HEADS):
        cols = slice(h * GDN_DV, (h + 1) * GDN_DV)
        o = of_ref[:, cols] + ob_ref[:, cols]
        on = o * lax.rsqrt(jnp.mean(o * o, axis=-1, keepdims=True) + EPS) * gn_ref[...]
        gdn_ref[:, cols] = (on * _silu(z_ref[:, cols].astype(F32))).astype(BF16)
    y = x_ref[...] + _dot(mla_ref[...], wo_ref[:MLA_WIDTH, :]) + _dot(gdn_ref[...], wo_ref[MLA_WIDTH:, :])
    x1_ref[...] = y
    xn = y * lax.rsqrt(jnp.mean(y * y, axis=-1, keepdims=True) + EPS) * g_ref[...]
    xn_ref[...] = xn
    logits = _dot3(xn, wr_ref[...])
    e = jnp.exp(logits - jnp.max(logits, axis=-1, keepdims=True))
    aff_ref[...] = e / jnp.sum(e, axis=-1, keepdims=True)


def _out_proj(x2d, mla, o_fwd, o_bwd, oa, gn, wo, g2, wr):
    t = x2d.shape[0]
    tm = min(256, t)
    rows = lambda w: pl.BlockSpec((tm, w), lambda i: (i, 0))
    return pl.pallas_call(
        _out_proj_kernel,
        grid=(t // tm,),
        in_specs=[rows(D_MODEL), rows(MLA_WIDTH), rows(GDN_WIDTH), rows(GDN_WIDTH), rows(GDN_WIDTH),
                  _resident(gn.shape), _resident(wo.shape), _resident(g2.shape), _resident(wr.shape)],
        out_specs=[rows(D_MODEL), rows(D_MODEL), rows(N_EXPERTS)],
        out_shape=[
            jax.ShapeDtypeStruct((t, D_MODEL), F32),
            jax.ShapeDtypeStruct((t, D_MODEL), F32),
            jax.ShapeDtypeStruct((t, N_EXPERTS), F32),
        ],
        scratch_shapes=[pltpu.VMEM((tm, GDN_WIDTH), BF16)],
        compiler_params=_params(("parallel",), 48),
        name="out_proj",
    )(x2d, mla, o_fwd, o_bwd, oa, gn, wo, g2, wr)


def _segment_count(flags, seg_lower, tri_upper):
    within = _dot(flags.astype(BF16), tri_upper)
    total = jnp.broadcast_to(within[:, ROUTE_BLOCK - 1:], within.shape)
    before = _dot(seg_lower, total.astype(BF16))
    return within + before, before, total


def _route_select_kernel(aff_ref, slot_ref, before_ref, total_ref, *, cap):
    n_e, nb, _ = aff_ref.shape
    rows = n_e * nb
    aff = aff_ref[...]

    def count_ge(t):
        return jnp.sum((aff >= t).astype(F32), axis=(1, 2), keepdims=True)

    e_hi = jnp.full((n_e, 1, 1), -1.0, F32)
    e_lo = jnp.full((n_e, 1, 1), float(EXP_SPAN), F32)
    for _ in range(EXP_SPAN.bit_length()):
        e_mid = jnp.floor((e_hi + e_lo) * 0.5)
        enough = count_ge(jnp.exp2(-e_mid)) >= cap
        e_lo = jnp.where(enough, e_mid, e_lo)
        e_hi = jnp.where(enough, e_hi, e_mid)
    lo = jnp.where(e_lo >= EXP_SPAN, 0.0, jnp.exp2(-e_lo))
    hi = jnp.exp2(-e_hi)
    for _ in range(MANTISSA_STEPS):
        mid = lo + (hi - lo) * 0.5
        enough = count_ge(mid) >= cap
        lo = jnp.where(enough, mid, lo)
        hi = jnp.where(enough, hi, mid)
    above = aff >= hi
    tied = jnp.logical_and(aff >= lo, jnp.logical_not(above))
    need = cap - jnp.sum(above.astype(F32), axis=(1, 2), keepdims=True)

    ri = lax.broadcasted_iota(I32, (rows, rows), 0)
    ci = lax.broadcasted_iota(I32, (rows, rows), 1)
    nb_shift = nb.bit_length() - 1
    assert nb == 1 << nb_shift
    same_expert = lax.shift_right_logical(ri, nb_shift) == lax.shift_right_logical(ci, nb_shift)
    seg_lower = jnp.logical_and(same_expert, ci < ri).astype(BF16)
    tri_upper = (lax.broadcasted_iota(I32, (ROUTE_BLOCK, ROUTE_BLOCK), 0)
                 <= lax.broadcasted_iota(I32, (ROUTE_BLOCK, ROUTE_BLOCK), 1)).astype(BF16)

    tied_f = tied.astype(F32).reshape(rows, ROUTE_BLOCK)
    tied_rank = _segment_count(tied_f, seg_lower, tri_upper)[0] - tied_f
    chosen = jnp.logical_or(above, jnp.logical_and(tied, tied_rank.reshape(n_e, nb, ROUTE_BLOCK) < need))
    chosen_f = chosen.astype(F32).reshape(rows, ROUTE_BLOCK)
    incl, before, total = _segment_count(chosen_f, seg_lower, tri_upper)
    slot_ref[...] = jnp.where(chosen_f > 0, incl - 1.0, -1.0).astype(I32)
    before_ref[...] = before.astype(I32)
    total_ref[...] = total.astype(I32)


def _route_select(aff3, cap):
    n_e, nb, _ = aff3.shape
    rows = n_e * nb
    table = jax.ShapeDtypeStruct((rows, ROUTE_BLOCK), I32)
    return pl.pallas_call(
        functools.partial(_route_select_kernel, cap=cap),
        out_shape=[table, table, table],
        compiler_params=_params(None, 48),
        name="route_select",
    )(aff3)


def _route_index_kernel(slot_ref, aff_ref, before_ref, ends_ref, totals_ref, idx_ref, gate_ref):
    cap = idx_ref.shape[1]
    nb = slot_ref.shape[0]
    s_col = lax.broadcasted_iota(I32, (cap, nb), 0)
    done = ends_ref[0] <= s_col
    blk = jnp.sum(done.astype(F32), axis=1, keepdims=True)
    blk_first = jnp.sum(jnp.where(done, totals_ref[0].astype(F32), 0.0), axis=1, keepdims=True)
    onehot = (lax.broadcasted_iota(I32, (cap, nb), 1) == blk.astype(I32)).astype(BF16)
    slot = slot_ref[...]
    local = jnp.where(slot >= 0, slot - before_ref[...] + 1, 0).astype(BF16)
    picked = _dot(onehot, local)
    target = s_col[:, :1].astype(F32) - blk_first + 1.0
    match = picked == target
    lane = lax.broadcasted_iota(I32, match.shape, 1).astype(F32)
    tok = blk * ROUTE_BLOCK + jnp.sum(jnp.where(match, lane, 0.0), axis=1, keepdims=True)
    idx_ref[0] = tok.astype(I32)
    a = aff_ref[...]
    hi = a.astype(BF16)
    r1 = a - hi.astype(F32)
    mid = r1.astype(BF16)
    lo = (r1 - mid.astype(F32)).astype(BF16)
    a_rows = _dot(onehot, hi) + _dot(onehot, mid) + _dot(onehot, lo)
    gate_ref[0] = jnp.sum(jnp.where(match, a_rows, 0.0), axis=1, keepdims=True)


def _route_index(slot, aff2, before, ends, totals, cap):
    rows = slot.shape[0]
    n_e = N_EXPERTS
    nb = rows // n_e
    blocks = pl.BlockSpec((nb, ROUTE_BLOCK), lambda e: (e, 0))
    row = pl.BlockSpec((1, 1, nb), lambda e: (e, 0, 0))
    out = pl.BlockSpec((1, cap, 1), lambda e: (e, 0, 0))
    return pl.pallas_call(
        _route_index_kernel,
        grid=(n_e,),
        in_specs=[blocks, blocks, blocks, row, row],
        out_specs=[out, out],
        out_shape=[jax.ShapeDtypeStruct((n_e, cap, 1), I32), jax.ShapeDtypeStruct((n_e, cap, 1), F32)],
        compiler_params=_params(("parallel",), 48),
        name="route_index",
    )(slot, aff2, before, ends, totals)


def _ffn_kernel(idx_ref, xn_hbm, gate_ref, wg_ref, wu_ref, wd_ref, ye_ref, stage_ref, xe_ref, acc_ref, sem):
    e = pl.program_id(0)
    f = pl.program_id(1)
    cap = xe_ref.shape[0]

    def start_gather(expert, buf):
        def start(group, c):
            for j in range(SUBLANES):
                tok = idx_ref[expert * cap + group * SUBLANES + j]
                src = xn_hbm.at[lax.shift_right_logical(tok, 3), pl.ds(jnp.bitwise_and(tok, SUBLANES - 1), 1), :]
                pltpu.make_async_copy(src, stage_ref.at[buf, group, pl.ds(j, 1), :], sem.at[buf]).start()
            return c

        lax.fori_loop(0, cap // SUBLANES, start, 0)

    @pl.when(f == 0)
    def _():
        @pl.when(e == 0)
        def _():
            start_gather(0, 0)

        buf = e % 2
        pltpu.make_async_copy(xn_hbm.at[pl.ds(0, cap // SUBLANES)], stage_ref.at[buf], sem.at[buf]).wait()
        xe_ref[...] = stage_ref[buf].reshape(cap, D_MODEL).astype(BF16)
        acc_ref[...] = jnp.zeros_like(acc_ref)

        @pl.when(e + 1 < pl.num_programs(0))
        def _():
            start_gather(e + 1, 1 - buf)

    xe = xe_ref[...]
    g = _dot(xe, wg_ref[0].astype(BF16))
    u = _dot(xe, wu_ref[0].astype(BF16))
    h = (_silu(g) * u).astype(BF16)
    acc_ref[...] += _dot(h, wd_ref[0].astype(BF16))

    @pl.when(f == pl.num_programs(1) - 1)
    def _():
        ye_ref[0] = (acc_ref[...] * gate_ref[0]).astype(BF16)


def _ffn(idx, xn, gates3, w_gate, w_up, w_down):
    e, cap = idx.shape
    tf = 256
    grid_spec = pltpu.PrefetchScalarGridSpec(
        num_scalar_prefetch=1,
        grid=(e, EXPERT_FF // tf),
        in_specs=[
            pl.BlockSpec(memory_space=pl.ANY),
            pl.BlockSpec((1, cap, 1), lambda ee, f, ix: (ee, 0, 0)),
            pl.BlockSpec((1, D_MODEL, tf), lambda ee, f, ix: (ee, 0, f)),
            pl.BlockSpec((1, D_MODEL, tf), lambda ee, f, ix: (ee, 0, f)),
            pl.BlockSpec((1, tf, D_MODEL), lambda ee, f, ix: (ee, f, 0)),
        ],
        out_specs=pl.BlockSpec((1, cap, D_MODEL), lambda ee, f, ix: (ee, 0, 0)),
        scratch_shapes=[
            pltpu.VMEM((2, cap // SUBLANES, SUBLANES, D_MODEL), F32),
            pltpu.VMEM((cap, D_MODEL), BF16),
            pltpu.VMEM((cap, D_MODEL), F32),
            pltpu.SemaphoreType.DMA((2,)),
        ],
    )
    return pl.pallas_call(
        _ffn_kernel,
        grid_spec=grid_spec,
        out_shape=jax.ShapeDtypeStruct((e, cap, D_MODEL), BF16),
        compiler_params=_params(("arbitrary", "arbitrary"), 56),
        name="moe_ffn",
    )(idx.reshape(-1), xn.reshape(-1, SUBLANES, D_MODEL), gates3, w_gate, w_up, w_down)


def _combine_kernel(lo_ref, cnt_ref, slot_ref, x1_ref, lo_row_ref, cnt_row_ref, lo_col_ref, cnt_col_ref, ye_hbm,
                    y_ref, stage_ref, sem):
    i = pl.program_id(0)
    nt = pl.num_programs(0)
    n_e = ye_hbm.shape[0]
    tt = slot_ref.shape[0]
    shift = BF16_ROWS.bit_length() - 1

    def plan(tile):
        entries = []
        base = jnp.int32(0)
        for e in range(n_e):
            lo = lo_ref[tile * n_e + e]
            cnt = cnt_ref[tile * n_e + e]
            first = lax.shift_right_logical(lo, shift)
            groups = jnp.where(cnt > 0, lax.shift_right_logical(lo + cnt + (BF16_ROWS - 1), shift) - first, 0)
            entries.append((first * BF16_ROWS, groups, base))
            base = base + groups * BF16_ROWS
        return entries, base

    def for_each_group(tile, buf, fn):
        entries, _ = plan(tile)
        for e, (a0, groups, base) in enumerate(entries):
            def body(j, c, e=e, a0=a0, base=base):
                src = ye_hbm.at[e, pl.ds(pl.multiple_of(a0 + j * BF16_ROWS, BF16_ROWS), BF16_ROWS), :]
                dst = stage_ref.at[buf, pl.ds(pl.multiple_of(base + j * BF16_ROWS, BF16_ROWS), BF16_ROWS), :]
                fn(pltpu.make_async_copy(src, dst, sem.at[buf]))
                return c

            lax.fori_loop(0, groups, body, 0)

    buf = i % 2

    @pl.when(i == 0)
    def _():
        stage_ref[...] = jnp.zeros_like(stage_ref)
        for_each_group(0, 0, lambda cp: cp.start())

    @pl.when(i + 1 < nt)
    def _():
        for_each_group(i + 1, 1 - buf, lambda cp: cp.start())

    for_each_group(i, buf, lambda cp: cp.wait())

    def staged(lo, cnt):
        first = lax.shift_right_logical(lo, shift)
        groups = jnp.where(cnt > 0, lax.shift_right_logical(lo + cnt + (BF16_ROWS - 1), shift) - first, 0)
        return first * BF16_ROWS, groups * BF16_ROWS

    a0_row, rows_row = staged(lo_row_ref[0], cnt_row_ref[0])
    _, rows_col = staged(lo_col_ref[0], cnt_col_ref[0])
    ei = lax.broadcasted_iota(I32, (n_e, n_e), 0)
    ej = lax.broadcasted_iota(I32, (n_e, n_e), 1)
    base_row = _dot(rows_row.astype(F32).astype(BF16), (ei < ej).astype(BF16)).astype(I32)
    base_col = _dot((ej < ei).astype(BF16),
                    jnp.broadcast_to(rows_col, (n_e, LANE)).astype(F32).astype(BF16))[:, :1].astype(I32)
    slot = slot_ref[...]
    rel = jnp.where(slot >= 0, slot - a0_row + base_row, -1)
    rel_hi = lax.shift_right_arithmetic(rel, 6).astype(F32).astype(BF16)
    rel_lo = jnp.bitwise_and(rel, 63).astype(F32).astype(BF16)
    _, total = plan(i)
    y_ref[...] = x1_ref[...]

    def k_block(kb, c):
        first = kb * COMBINE_K
        col_e = lax.broadcasted_iota(I32, (n_e, COMBINE_K), 1) + first
        owner = jnp.logical_and(col_e >= base_col, col_e < base_col + rows_col).astype(BF16)
        want = 64.0 * _dot(rel_hi, owner) + _dot(rel_lo, owner)
        col_t = (lax.broadcasted_iota(I32, (tt, COMBINE_K), 1) + first).astype(F32)
        rows = pl.ds(pl.multiple_of(first, COMBINE_K), COMBINE_K)
        y_ref[...] += _dot((want == col_t).astype(BF16), stage_ref[buf, rows, :])
        return c

    lax.fori_loop(0, lax.shift_right_logical(total + (COMBINE_K - 1), COMBINE_K.bit_length() - 1), k_block, 0)


def _combine(lo_flat, cnt_flat, slot, x1, ye):
    t = x1.shape[0]
    n_e = ye.shape[0]
    tt = ROUTE_BLOCK
    stage_rows = n_e * (tt + BF16_ROWS)
    assert stage_rows % COMBINE_K == 0
    nt = t // tt
    lo2 = lo_flat.reshape(nt, n_e)
    cnt2 = cnt_flat.reshape(nt, n_e)
    row_spec = pl.BlockSpec((1, 1, n_e), lambda i, lo, cnt: (i, 0, 0))
    col_spec = pl.BlockSpec((1, n_e, 1), lambda i, lo, cnt: (i, 0, 0))
    grid_spec = pltpu.PrefetchScalarGridSpec(
        num_scalar_prefetch=2,
        grid=(nt,),
        in_specs=[
            pl.BlockSpec((tt, n_e), lambda i, lo, cnt: (i, 0)),
            pl.BlockSpec((tt, D_MODEL), lambda i, lo, cnt: (i, 0)),
            row_spec, row_spec, col_spec, col_spec,
            pl.BlockSpec(memory_space=pl.ANY),
        ],
        out_specs=pl.BlockSpec((tt, D_MODEL), lambda i, lo, cnt: (i, 0)),
        scratch_shapes=[
            pltpu.VMEM((2, stage_rows, D_MODEL), BF16),
            pltpu.SemaphoreType.DMA((2,)),
        ],
    )
    return pl.pallas_call(
        _combine_kernel,
        grid_spec=grid_spec,
        out_shape=jax.ShapeDtypeStruct((t, D_MODEL), F32),
        compiler_params=_params(("arbitrary",), 48),
        name="moe_combine",
    )(lo_flat, cnt_flat, slot, x1, lo2[:, None, :], cnt2[:, None, :], lo2[:, :, None], cnt2[:, :, None], ye)


def _rope_tables(s):
    half = QK_ROPE // 2
    inv_freq = ROPE_THETA ** (-jnp.arange(half, dtype=F32) / half)
    ang = jnp.arange(s, dtype=F32)[:, None] * inv_freq[None, :]
    cos = jnp.cos(ang)
    sin = jnp.sin(ang)
    zeros = jnp.zeros((s, LANE - QK_ROPE), F32)
    return jnp.concatenate([cos, cos, zeros], axis=1), jnp.concatenate([-sin, sin, zeros], axis=1)


def _prepare_weights(norm1_g, w_in, q_a_norm_g, w_q_b, kv_a_norm_g, w_kv_b, q_norm_g, k_norm_g, conv_w,
                     a_log_fwd, a_log_bwd, dt_bias_fwd, dt_bias_bwd, gdn_norm_g, w_o, norm2_g, w_router):
    o_kv = Q_LORA
    o_kr = o_kv + KV_LORA
    o_qkv = o_kr + QK_ROPE
    o_z = o_qkv + QKV_GDN
    o_g = o_z + GDN_WIDTH
    wa = jnp.concatenate([w_in[:, o_z:o_g], w_in[:, :o_kr]], axis=1).astype(BF16)
    wb = jnp.concatenate(
        [w_in[:, o_qkv:o_z], w_in[:, o_kr:o_qkv], w_in[:, o_g:], jnp.zeros((D_MODEL, LANE - QK_ROPE - 4 * GDN_HEADS), F32)],
        axis=1).astype(BF16)
    wq = jnp.pad(w_q_b.reshape(Q_LORA, MLA_HEADS, QK_HEAD), ((0, 0), (0, 0), (0, QK_PAD - QK_HEAD)))
    wq = wq.reshape(Q_LORA, MLA_HEADS * QK_PAD).astype(BF16)
    pad_g = lambda g: jnp.pad(g, (0, QK_PAD - QK_HEAD)).reshape(1, QK_PAD)
    alog = jnp.concatenate([a_log_fwd, a_log_bwd]).reshape(2 * GDN_HEADS, 1, 1)
    dtb = jnp.concatenate([dt_bias_fwd, dt_bias_bwd]).reshape(2 * GDN_HEADS, 1, 1)
    return dict(
        g1=norm1_g.reshape(1, D_MODEL), wa=wa, wb=wb,
        gqa=q_a_norm_g.reshape(1, Q_LORA), gkva=kv_a_norm_g.reshape(1, KV_LORA),
        wq=wq, wkv=w_kv_b.astype(BF16), gq=pad_g(q_norm_g), gk=pad_g(k_norm_g),
        conv_w=conv_w,
        alog=jnp.broadcast_to(alog, (2 * GDN_HEADS, 1, LANE)), dtb=jnp.broadcast_to(dtb, (2 * GDN_HEADS, 1, LANE)),
        gn=gdn_norm_g.reshape(1, GDN_DV), wo=w_o.astype(BF16), g2=norm2_g.reshape(1, D_MODEL), wr=w_router,
    )


def _route(aff, cap):
    t, n_e = aff.shape
    nb = t // ROUTE_BLOCK
    aff3 = aff.T.reshape(n_e, nb, ROUTE_BLOCK)
    slot, before, total = _route_select(aff3, cap)
    before_b = before[:, 0].reshape(n_e, nb)
    total_b = total[:, 0].reshape(n_e, nb)
    idx, gates = _route_index(slot, aff3.reshape(n_e * nb, ROUTE_BLOCK), before,
                              (before_b + total_b).reshape(n_e, 1, nb), total_b.reshape(n_e, 1, nb), cap)
    slot_by_token = slot.reshape(n_e, t).T
    return idx.reshape(n_e, cap), gates, slot_by_token, before_b.T.reshape(-1), total_b.T.reshape(-1)


def _encoder_layer(x, p, w_gate, w_up, w_down):
    bt, s, _ = x.shape
    t = bt * s
    x2d = x.reshape(t, D_MODEL)
    oa, ob = _in_proj(x2d, p["g1"], p["wa"], p["wb"])

    cos, sin = _rope_tables(s)
    q, k, v = _mla_prep(oa, ob, cos, sin, p["gqa"], p["gkva"], p["wq"], p["wkv"], p["gq"], p["gk"], bt, s)
    mla_out = _attention(q, k, v)

    ob3 = ob.reshape(bt, s, NB)
    qkv_h = _conv(ob3, p["conv_w"])
    g_raw = ob3[:, :, QKV_GDN + QK_ROPE:QKV_GDN + QK_ROPE + 4 * GDN_HEADS]
    g_t = jnp.swapaxes(g_raw, 1, 2).reshape(bt, 4 * GDN_HEADS, s // LANE, LANE)
    g_out = _gates(g_t, p["alog"], p["dtb"]).reshape(bt, 4, GDN_HEADS, s)
    g_rows = g_out[:, :2].reshape(bt, 2 * GDN_HEADS, s // CHUNK, CHUNK)
    g_cols = jnp.transpose(g_out, (0, 2, 3, 1))
    o_fwd, o_bwd = _gdn_scan(_gdn_prep(qkv_h, g_rows, g_cols), g_rows)

    x1, xn, aff = _out_proj(x2d, mla_out.reshape(t, MLA_WIDTH), o_fwd.reshape(t, GDN_WIDTH),
                            o_bwd.reshape(t, GDN_WIDTH), oa, p["gn"], p["wo"], p["g2"], p["wr"])

    cap = max(1, CAPACITY_FACTOR * t // N_EXPERTS)
    idx, gates, slot, lo, cnt = _route(aff, cap)
    ye = _ffn(idx, xn, gates, w_gate, w_up, w_down)
    y = _combine(lo, cnt, slot, x1, ye)
    return y.reshape(bt, s, D_MODEL)


def kernel(x_prompt, x_sample, norm1_g, w_in, q_a_norm_g, w_q_b, kv_a_norm_g, w_kv_b, q_norm_g, k_norm_g, conv_w,
           a_log_fwd, a_log_bwd, dt_bias_fwd, dt_bias_bwd, gdn_norm_g, w_o, norm2_g, w_router, w_gate, w_up, w_down):
    y_prompt = x_prompt
    y_sample = x_sample
    for l in range(norm1_g.shape[0]):
        p = _prepare_weights(norm1_g[l], w_in[l], q_a_norm_g[l], w_q_b[l], kv_a_norm_g[l], w_kv_b[l], q_norm_g[l],
                             k_norm_g[l], conv_w[l], a_log_fwd[l], a_log_bwd[l], dt_bias_fwd[l], dt_bias_bwd[l],
                             gdn_norm_g[l], w_o[l], norm2_g[l], w_router[l])
        y_prompt = _encoder_layer(y_prompt, p, w_gate[l], w_up[l], w_down[l])
        y_sample = _encoder_layer(y_sample, p, w_gate[l], w_up[l], w_down[l])
    return (y_prompt, y_sample)
```

```python
import functools
import math

import jax
import jax.numpy as jnp
from jax import lax
from jax.experimental import pallas as pl
from jax.experimental.pallas import tpu as pltpu

F32 = jnp.float32
BF16 = jnp.bfloat16
I32 = jnp.int32

D_MODEL = 2048
MLA_HEADS = 8
Q_LORA = 512
KV_LORA = 256
QK_NOPE = 128
QK_ROPE = 64
QK_HEAD = QK_NOPE + QK_ROPE
QK_PAD = 256
V_HEAD = 128
V_PAD = 256
ROPE_THETA = 10000.0
GDN_HEADS = 8
GDN_DK = 128
GDN_DV = 128
QKV_GDN = 2 * GDN_HEADS * GDN_DK + GDN_HEADS * GDN_DV
GDN_WIDTH = GDN_HEADS * GDN_DV
MLA_WIDTH = MLA_HEADS * V_HEAD
CONV_W = 5
CHUNK = 64
N_EXPERTS = 16
EXPERT_FF = 1024
CAPACITY_FACTOR = 2
EPS = 1e-6

LANE = 128
SUBLANES = 8
BF16_ROWS = 16
MIB = 1024 * 1024

NA = GDN_WIDTH + Q_LORA + KV_LORA
NB = QKV_GDN + LANE
QLAT_COL_BLOCK = GDN_WIDTH // Q_LORA
KVLAT_COL_BLOCK = (GDN_WIDTH + Q_LORA) // KV_LORA
MISC_COL_BLOCK = QKV_GDN // LANE
PREP_CHUNKS = 8
ATTN_SUB = 256
ROUTE_BLOCK = LANE
COMBINE_K = 256
EXP_SPAN = 126
MANTISSA_STEPS = 40

NN = (((1,), (0,)), ((), ()))
NT = (((1,), (1,)), ((), ()))
TN = (((0,), (0,)), ((), ()))


def _params(semantics, vmem_mib):
    return pltpu.CompilerParams(dimension_semantics=semantics, vmem_limit_bytes=vmem_mib * MIB)


def _resident(shape):
    nd = len(shape)
    return pl.BlockSpec(shape, lambda *_: (0,) * nd, pipeline_mode=pl.Buffered(1))


def _dot(a, b, dims=NN):
    return lax.dot_general(a, b, dims, preferred_element_type=F32)


def _dotb(a, b, dims=NN):
    return lax.dot_general(a.astype(BF16), b.astype(BF16), dims, preferred_element_type=F32)


def _split(x):
    hi = x.astype(BF16)
    lo = (x - hi.astype(F32)).astype(BF16)
    return hi, lo


def _dot3(a, b, dims=NN):
    ah, al = _split(a)
    bh, bl = _split(b)
    return _dot(ah, bh, dims) + (_dot(ah, bl, dims) + _dot(al, bh, dims))


def _silu(x):
    return x * jax.nn.sigmoid(x)


def _in_proj_kernel(x_ref, g_ref, wa_ref, wb_ref, oa_ref, ob_ref):
    x = x_ref[...]
    ms = jnp.mean(x * x, axis=-1, keepdims=True)
    h = (x * lax.rsqrt(ms + EPS) * g_ref[...]).astype(BF16)
    oa_ref[...] = _dot(h, wa_ref[...]).astype(BF16)
    ob_ref[...] = _dot(h, wb_ref[...])


def _in_proj(x2d, g, wa, wb):
    t = x2d.shape[0]
    tm = min(512, t)
    return pl.pallas_call(
        _in_proj_kernel,
        grid=(t // tm,),
        in_specs=[
            pl.BlockSpec((tm, D_MODEL), lambda i: (i, 0)),
            _resident((1, D_MODEL)),
            _resident(wa.shape),
            _resident(wb.shape),
        ],
        out_specs=[
            pl.BlockSpec((tm, NA), lambda i: (i, 0)),
            pl.BlockSpec((tm, NB), lambda i: (i, 0)),
        ],
        out_shape=[
            jax.ShapeDtypeStruct((t, NA), BF16),
            jax.ShapeDtypeStruct((t, NB), F32),
        ],
        compiler_params=_params(("parallel",), 56),
        name="in_proj",
    )(x2d, g, wa, wb)


def _mla_prep_kernel(ql_ref, kvl_ref, misc_ref, cos_ref, sin_ref, gqa_ref, gkva_ref, wq_ref, wkv_ref,
                     gq_ref, gk_ref, q_ref, k_ref, v_ref):
    def norm(x, g):
        return x * lax.rsqrt(jnp.mean(x * x, axis=-1, keepdims=True) + EPS) * g

    qn = norm(ql_ref[...].astype(F32), gqa_ref[...]).astype(BF16)
    kvn = norm(kvl_ref[...].astype(F32), gkva_ref[...]).astype(BF16)
    q = _dot(qn, wq_ref[...])
    kv = _dot(kvn, wkv_ref[...])

    cos = cos_ref[...]
    sin = sin_ref[...]
    lane = lax.broadcasted_iota(I32, cos.shape, 1)

    def rope(t):
        rot = jnp.where(lane < QK_ROPE // 2, pltpu.roll(t, LANE - QK_ROPE // 2, 1), pltpu.roll(t, QK_ROPE // 2, 1))
        return t * cos + rot * sin

    gq = gq_ref[...]
    gk = gk_ref[...]
    kr = jnp.where(lane < QK_ROPE, misc_ref[...], 0.0)
    kr_ssq = jnp.sum(kr * kr, axis=-1, keepdims=True)
    kr_rot = rope(kr * gk[:, QK_NOPE:])
    q_scale = QK_HEAD ** -0.5
    ones_col = jnp.where(lane == 0, 1.0, 0.0).astype(BF16)
    for h in range(MLA_HEADS):
        qh = q[:, h * QK_PAD:(h + 1) * QK_PAD]
        sc = lax.rsqrt(jnp.sum(qh * qh, axis=-1, keepdims=True) * (1.0 / QK_HEAD) + EPS) * q_scale
        q_ref[0, h, :, :QK_NOPE] = (qh[:, :QK_NOPE] * gq[:, :QK_NOPE] * sc).astype(BF16)
        q_ref[0, h, :, QK_NOPE:] = (rope(qh[:, QK_NOPE:] * gq[:, QK_NOPE:]) * sc).astype(BF16)
        kn = kv[:, h * 256:h * 256 + QK_NOPE]
        sk = lax.rsqrt((jnp.sum(kn * kn, axis=-1, keepdims=True) + kr_ssq) * (1.0 / QK_HEAD) + EPS)
        k_ref[0, h, :, :QK_NOPE] = (kn * gk[:, :QK_NOPE] * sk).astype(BF16)
        k_ref[0, h, :, QK_NOPE:] = (kr_rot * sk).astype(BF16)
        v_ref[0, h, :, :V_HEAD] = kv[:, h * 256 + QK_NOPE:(h + 1) * 256].astype(BF16)
        v_ref[0, h, :, V_HEAD:] = ones_col


def _mla_prep(oa, ob, cos, sin, gqa, gkva, wq, wkv, gq, gk, bt, s):
    tm = min(512, s)
    nt = s // tm
    return pl.pallas_call(
        _mla_prep_kernel,
        grid=(bt, nt),
        in_specs=[
            pl.BlockSpec((tm, Q_LORA), lambda b, i: (b * nt + i, QLAT_COL_BLOCK)),
            pl.BlockSpec((tm, KV_LORA), lambda b, i: (b * nt + i, KVLAT_COL_BLOCK)),
            pl.BlockSpec((tm, LANE), lambda b, i: (b * nt + i, MISC_COL_BLOCK)),
            pl.BlockSpec((tm, LANE), lambda b, i: (i, 0)),
            pl.BlockSpec((tm, LANE), lambda b, i: (i, 0)),
            _resident(gqa.shape),
            _resident(gkva.shape),
            _resident(wq.shape),
            _resident(wkv.shape),
            _resident(gq.shape),
            _resident(gk.shape),
        ],
        out_specs=[
            pl.BlockSpec((1, MLA_HEADS, tm, QK_PAD), lambda b, i: (b, 0, i, 0)),
            pl.BlockSpec((1, MLA_HEADS, tm, QK_PAD), lambda b, i: (b, 0, i, 0)),
            pl.BlockSpec((1, MLA_HEADS, tm, V_PAD), lambda b, i: (b, 0, i, 0)),
        ],
        out_shape=[
            jax.ShapeDtypeStruct((bt, MLA_HEADS, s, QK_PAD), BF16),
            jax.ShapeDtypeStruct((bt, MLA_HEADS, s, QK_PAD), BF16),
            jax.ShapeDtypeStruct((bt, MLA_HEADS, s, V_PAD), BF16),
        ],
        compiler_params=_params(("parallel", "parallel"), 48),
        name="mla_prep",
    )(oa, oa, ob, cos, sin, gqa, gkva, wq, wkv, gq, gk)


def _attn_kernel(q_ref, k_ref, v_ref, o_ref):
    k = k_ref[0, 0]
    v = v_ref[0, 0]
    parts = [slice(r, r + ATTN_SUB) for r in range(0, q_ref.shape[2], ATTN_SUB)]
    scores = [_dot(q_ref[0, 0, rows, :], k, NT) for rows in parts]
    for rows, s in zip(parts, scores):
        m = jnp.max(s, axis=-1, keepdims=True)
        p = jnp.exp((s - m).astype(BF16))
        o = _dot(p, v)
        o_ref[0, rows, :] = (o[:, :V_HEAD] / o[:, V_HEAD:V_HEAD + 1]).astype(BF16)


def _attention(q, k, v):
    bt, h, s, _ = q.shape
    tq = min(4 * ATTN_SUB, s)
    return pl.pallas_call(
        _attn_kernel,
        grid=(bt, h, s // tq),
        in_specs=[
            pl.BlockSpec((1, 1, tq, QK_PAD), lambda b, hh, i: (b, hh, i, 0)),
            pl.BlockSpec((1, 1, s, QK_PAD), lambda b, hh, i: (b, hh, 0, 0)),
            pl.BlockSpec((1, 1, s, V_PAD), lambda b, hh, i: (b, hh, 0, 0)),
        ],
        out_specs=pl.BlockSpec((1, tq, V_HEAD), lambda b, hh, i: (b, i, hh)),
        out_shape=jax.ShapeDtypeStruct((bt, s, MLA_WIDTH), BF16),
        compiler_params=_params(("parallel", "parallel", "parallel"), 48),
        name="attention",
    )(q, k, v)


def _conv_kernel(x_ref, w_ref, o_ref, pad_ref):
    c = pl.program_id(1)
    x = x_ref[0]
    s = x.shape[0]
    w = w_ref[...]
    pad = CONV_W // 2
    halo = jnp.zeros((SUBLANES, LANE), F32)
    pad_ref[:SUBLANES, :] = halo
    pad_ref[SUBLANES + s:, :] = halo
    pad_ref[SUBLANES:SUBLANES + s, :] = x
    acc = x * w[pad:pad + 1]
    for d in range(-pad, pad + 1):
        if d != 0:
            acc = acc + pad_ref[SUBLANES + d:SUBLANES + d + s, :] * w[pad + d:pad + d + 1]
    y = _silu(acc)
    inv = lax.rsqrt(jnp.sum(y * y, axis=-1, keepdims=True) + EPS)
    is_q = c < GDN_HEADS
    is_qk = c < 2 * GDN_HEADS
    o_ref[0, 0] = y * (jnp.where(is_qk, inv, 1.0) * jnp.where(is_q, GDN_DK ** -0.5, 1.0))


def _conv(ob3, conv_w):
    bt, s, _ = ob3.shape
    nblk = QKV_GDN // LANE
    return pl.pallas_call(
        _conv_kernel,
        grid=(bt, nblk),
        in_specs=[
            pl.BlockSpec((1, s, LANE), lambda b, c: (b, 0, c)),
            pl.BlockSpec((CONV_W, LANE), lambda b, c: (0, c)),
        ],
        out_specs=pl.BlockSpec((1, 1, s, LANE), lambda b, c: (b, c, 0, 0)),
        out_shape=jax.ShapeDtypeStruct((bt, nblk, s, LANE), F32),
        scratch_shapes=[pltpu.VMEM((s + 2 * SUBLANES, LANE), F32)],
        compiler_params=_params(("parallel", "parallel"), 48),
        name="gdn_conv",
    )(ob3, conv_w)


def _gates_kernel(x_ref, alog_ref, dtb_ref, o_ref):
    x = x_ref[0]
    nh = GDN_HEADS
    a = x[:2 * nh] + dtb_ref[...]
    softplus = jnp.maximum(a, 0.0) + jnp.log1p(jnp.exp(-jnp.abs(a)))
    g = -jnp.exp(alog_ref[...]) * softplus
    beta = jax.nn.sigmoid(x[2 * nh:])
    gf = g[:nh]
    gb = g[nh:]
    pos = lax.broadcasted_iota(I32, gf.shape, 2) % CHUNK
    sh = 1
    while sh < CHUNK:
        gf = gf + jnp.where(pos >= sh, pltpu.roll(gf, sh, 2), 0.0)
        gb = gb + jnp.where(pos < CHUNK - sh, pltpu.roll(gb, LANE - sh, 2), 0.0)
        sh *= 2
    o_ref[0, :nh] = gf
    o_ref[0, nh:2 * nh] = gb
    o_ref[0, 2 * nh:] = beta


def _gates(gt, alog, dtb):
    bt, r, sl, _ = gt.shape
    return pl.pallas_call(
        _gates_kernel,
        grid=(bt,),
        in_specs=[
            pl.BlockSpec((1, r, sl, LANE), lambda b: (b, 0, 0, 0)),
            _resident(alog.shape),
            _resident(dtb.shape),
        ],
        out_specs=pl.BlockSpec((1, r, sl, LANE), lambda b: (b, 0, 0, 0)),
        out_shape=jax.ShapeDtypeStruct(gt.shape, F32),
        compiler_params=_params(("parallel",), 32),
        name="gdn_gates",
    )(gt, alog, dtb)


def _unit_tri_inverses(lmats, eye):
    ps = [eye - l for l in lmats]
    ms = [_dotb(l, l) for l in lmats]
    n = 2
    while n < CHUNK // 2:
        ps = [p + _dotb(p, m) for p, m in zip(ps, ms)]
        ms = [_dotb(m, m) for m in ms]
        n *= 2
    return [p + _dotb(p, m) for p, m in zip(ps, ms)]


def _gdn_prep_kernel(q_ref, k_ref, v_ref, grf_ref, grb_ref, gc_ref,
                     wqf_ref, uf_ref, kdf_ref, inf_ref, wqb_ref, ub_ref, kdb_ref, inb_ref):
    ri = lax.broadcasted_iota(I32, (CHUNK, CHUNK), 0)
    ci = lax.broadcasted_iota(I32, (CHUNK, CHUNK), 1)
    eye = (ri == ci).astype(F32)
    dirs = (
        (0, grf_ref, ri >= ci, ri > ci, CHUNK - 1, (wqf_ref, uf_ref, kdf_ref, inf_ref)),
        (1, grb_ref, ri <= ci, ri < ci, 0, (wqb_ref, ub_ref, kdb_ref, inb_ref)),
    )
    lmats = []
    work = []
    for c in range(PREP_CHUNKS):
        rows = slice(c * CHUNK, (c + 1) * CHUNK)
        q = q_ref[0, 0, rows, :]
        k = k_ref[0, 0, rows, :]
        v = v_ref[0, 0, rows, :]
        g4 = gc_ref[0, 0, rows, :]
        kk = _dotb(k, k, NT)
        qk = _dotb(q, k, NT)
        for col, grow_ref, incl, strict, last_row, outs in dirs:
            gcol = g4[:, col:col + 1]
            beta = g4[:, 2 + col:3 + col]
            decay = jnp.exp(jnp.where(incl, gcol - grow_ref[0, 0, c:c + 1, :], -jnp.inf))
            lmats.append(jnp.where(strict, beta * kk * decay, 0.0))
            e_gc = jnp.exp(gcol)
            rhs = jnp.concatenate([v * beta, k * (beta * e_gc)], axis=-1)
            k_dec = k * jnp.exp(g4[last_row:last_row + 1, col:col + 1] - gcol)
            intra = jnp.where(incl, qk * decay, 0.0)
            work.append((c, rows, rhs, q * e_gc, k_dec, intra, outs))
    tinvs = _unit_tri_inverses(lmats, eye)
    for tinv, (c, rows, rhs, q_dec, k_dec, intra, (wq_ref, u_ref, kd_ref, in_ref)) in zip(tinvs, work):
        sol = _dotb(tinv, rhs)
        u_ref[0, 0, rows, :] = sol[:, :GDN_DV].astype(BF16)
        wq_ref[0, 0, c, :CHUNK, :] = sol[:, GDN_DV:].astype(BF16)
        wq_ref[0, 0, c, CHUNK:, :] = q_dec.astype(BF16)
        kd_ref[0, 0, rows, :] = k_dec.astype(BF16)
        in_ref[0, 0, rows, :] = intra.astype(BF16)


def _gdn_prep(qkv_h, g_rows, g_cols):
    bt, _, s, _ = qkv_h.shape
    nh = GDN_HEADS
    rb = PREP_CHUNKS * CHUNK
    n_chunks = s // CHUNK
    head = lambda off: (lambda b, h, i: (b, off + h, i, 0))
    seq_spec = lambda w: pl.BlockSpec((1, 1, rb, w), head(0))
    wq_spec = pl.BlockSpec((1, 1, PREP_CHUNKS, 2 * CHUNK, GDN_DK), lambda b, h, i: (b, h, i, 0, 0))
    out_specs = [wq_spec, seq_spec(GDN_DV), seq_spec(GDN_DK), seq_spec(CHUNK)]
    out_shape = [
        jax.ShapeDtypeStruct((bt, nh, n_chunks, 2 * CHUNK, GDN_DK), BF16),
        jax.ShapeDtypeStruct((bt, nh, s, GDN_DV), BF16),
        jax.ShapeDtypeStruct((bt, nh, s, GDN_DK), BF16),
        jax.ShapeDtypeStruct((bt, nh, s, CHUNK), BF16),
    ]
    return pl.pallas_call(
        _gdn_prep_kernel,
        grid=(bt, nh, s // rb),
        in_specs=[
            pl.BlockSpec((1, 1, rb, GDN_DK), head(0)),
            pl.BlockSpec((1, 1, rb, GDN_DK), head(nh)),
            pl.BlockSpec((1, 1, rb, GDN_DV), head(2 * nh)),
            pl.BlockSpec((1, 1, PREP_CHUNKS, CHUNK), head(0)),
            pl.BlockSpec((1, 1, PREP_CHUNKS, CHUNK), head(nh)),
            pl.BlockSpec((1, 1, rb, 4), head(0)),
        ],
        out_specs=out_specs + out_specs,
        out_shape=out_shape + out_shape,
        compiler_params=_params(("parallel", "parallel", "parallel"), 48),
        name="gdn_prep",
    )(qkv_h, qkv_h, qkv_h, g_rows, g_rows, g_cols)


def _gdn_scan_kernel(wqf_ref, uf_ref, kdf_ref, inf_ref, grf_ref, wqb_ref, ub_ref, kdb_ref, inb_ref, grb_ref,
                     of_ref, ob_ref, state_ref):
    @pl.when(pl.program_id(1) == 0)
    def _():
        state_ref[...] = jnp.zeros_like(state_ref)

    def step(c, carry):
        dirs = (
            (0, c, wqf_ref, uf_ref, kdf_ref, inf_ref, grf_ref, CHUNK - 1, of_ref),
            (1, PREP_CHUNKS - 1 - c, wqb_ref, ub_ref, kdb_ref, inb_ref, grb_ref, 0, ob_ref),
        )
        chains = [(h,) + d for h in range(GDN_HEADS) for d in dirs]
        states = [state_ref[d, h] for h, d, *_ in chains]
        rs = [_dot(wq_ref[0, h, cc], s.astype(BF16)) for (h, _, cc, wq_ref, *_), s in zip(chains, states)]
        for (h, d, cc, _, u_ref, kd_ref, in_ref, g_ref, last_lane, o_ref), s, r in zip(chains, states, rs):
            rows = pl.ds(pl.multiple_of(cc * CHUNK, CHUNK), CHUNK)
            v_new = (u_ref[0, h, rows, :].astype(F32) - r[:CHUNK]).astype(BF16)
            o = r[CHUNK:] + _dot(in_ref[0, h, rows, :], v_new)
            o_ref[0, rows, h * GDN_DV:(h + 1) * GDN_DV] = o.astype(BF16)
            g_last = g_ref[0, d * GDN_HEADS + h, pl.ds(cc, 1), :][:, last_lane:last_lane + 1]
            state_ref[d, h] = s * jnp.exp(g_last) + _dot(kd_ref[0, h, rows, :], v_new, TN)
        return carry

    lax.fori_loop(0, PREP_CHUNKS, step, 0)


def _gdn_scan(prep, g_rows):
    wqf, uf, kdf, inf_, wqb, ub, kdb, inb = prep
    bt, nh, s, _ = uf.shape
    rb = PREP_CHUNKS * CHUNK
    nblk = s // rb
    fwd = lambda b, i: (b, 0, i, 0)
    bwd = lambda b, i: (b, 0, nblk - 1 - i, 0)

    def specs(idx):
        idx5 = lambda b, i: idx(b, i) + (0,)
        return [
            pl.BlockSpec((1, nh, PREP_CHUNKS, 2 * CHUNK, GDN_DK), idx5),
            pl.BlockSpec((1, nh, rb, GDN_DV), idx),
            pl.BlockSpec((1, nh, rb, GDN_DK), idx),
            pl.BlockSpec((1, nh, rb, CHUNK), idx),
            pl.BlockSpec((1, 2 * nh, PREP_CHUNKS, CHUNK), idx),
        ]

    out = jax.ShapeDtypeStruct((bt, s, GDN_WIDTH), BF16)
    return pl.pallas_call(
        _gdn_scan_kernel,
        grid=(bt, nblk),
        in_specs=specs(fwd) + specs(bwd),
        out_specs=[
            pl.BlockSpec((1, rb, GDN_WIDTH), lambda b, i: (b, i, 0)),
            pl.BlockSpec((1, rb, GDN_WIDTH), lambda b, i: (b, nblk - 1 - i, 0)),
        ],
        out_shape=[out, out],
        scratch_shapes=[pltpu.VMEM((2, nh, GDN_DK, GDN_DV), F32)],
        compiler_params=_params(("parallel", "arbitrary"), 48),
        name="gdn_scan",
    )(wqf, uf, kdf, inf_, g_rows, wqb, ub, kdb, inb, g_rows)


def _out_proj_kernel(x_ref, mla_ref, of_ref, ob_ref, z_ref, gn_ref, wo_ref, g_ref, wr_ref, x1_ref, xn_ref, aff_ref,
                     gdn_ref):
    for h in range(GDN_HEADS):
        cols = slice(h * GDN_DV, (h + 1) * GDN_DV)
        o = of_ref[:, cols].astype(F32) + ob_ref[:, cols].astype(F32)
        on = o * lax.rsqrt(jnp.mean(o * o, axis=-1, keepdims=True) + EPS) * gn_ref[...]
        gdn_ref[:, cols] = (on * _silu(z_ref[:, cols].astype(F32))).astype(BF16)
    y = x_ref[...] + _dot(mla_ref[...], wo_ref[:MLA_WIDTH, :]) + _dot(gdn_ref[...], wo_ref[MLA_WIDTH:, :])
    x1_ref[...] = y
    xn = y * lax.rsqrt(jnp.mean(y * y, axis=-1, keepdims=True) + EPS) * g_ref[...]
    xn_ref[...] = xn
    logits = _dot3(xn, wr_ref[...])
    e = jnp.exp(logits - jnp.max(logits, axis=-1, keepdims=True))
    aff_ref[...] = e / jnp.sum(e, axis=-1, keepdims=True)


def _out_proj(x2d, mla, o_fwd, o_bwd, oa, gn, wo, g2, wr):
    t = x2d.shape[0]
    tm = min(256, t)
    rows = lambda w: pl.BlockSpec((tm, w), lambda i: (i, 0))
    return pl.pallas_call(
        _out_proj_kernel,
        grid=(t // tm,),
        in_specs=[rows(D_MODEL), rows(MLA_WIDTH), rows(GDN_WIDTH), rows(GDN_WIDTH), rows(GDN_WIDTH),
                  _resident(gn.shape), _resident(wo.shape), _resident(g2.shape), _resident(wr.shape)],
        out_specs=[rows(D_MODEL), rows(D_MODEL), rows(N_EXPERTS)],
        out_shape=[
            jax.ShapeDtypeStruct((t, D_MODEL), F32),
            jax.ShapeDtypeStruct((t, D_MODEL), F32),
            jax.ShapeDtypeStruct((t, N_EXPERTS), F32),
        ],
        scratch_shapes=[pltpu.VMEM((tm, GDN_WIDTH), BF16)],
        compiler_params=_params(("parallel",), 48),
        name="out_proj",
    )(x2d, mla, o_fwd, o_bwd, oa, gn, wo, g2, wr)


def _segment_count(flags, seg_lower, tri_upper):
    within = _dot(flags.astype(BF16), tri_upper)
    total = jnp.broadcast_to(within[:, ROUTE_BLOCK - 1:], within.shape)
    before = _dot(seg_lower, total.astype(BF16))
    return within + before, before, total


def _route_select_kernel(aff_ref, slot_ref, before_ref, total_ref, *, cap):
    n_e, nb, _ = aff_ref.shape
    rows = n_e * nb
    aff = aff_ref[...]

    def count_ge(t):
        return jnp.sum((aff >= t).astype(F32), axis=(1, 2), keepdims=True)

    e_hi = jnp.full((n_e, 1, 1), -1.0, F32)
    e_lo = jnp.full((n_e, 1, 1), float(EXP_SPAN), F32)
    for _ in range(EXP_SPAN.bit_length()):
        e_mid = jnp.floor((e_hi + e_lo) * 0.5)
        enough = count_ge(jnp.exp2(-e_mid)) >= cap
        e_lo = jnp.where(enough, e_mid, e_lo)
        e_hi = jnp.where(enough, e_hi, e_mid)
    lo = jnp.where(e_lo >= EXP_SPAN, 0.0, jnp.exp2(-e_lo))
    hi = jnp.exp2(-e_hi)
    for _ in range(MANTISSA_STEPS):
        mid = lo + (hi - lo) * 0.5
        enough = count_ge(mid) >= cap
        lo = jnp.where(enough, mid, lo)
        hi = jnp.where(enough, hi, mid)
    above = aff >= hi
    tied = jnp.logical_and(aff >= lo, jnp.logical_not(above))
    need = cap - jnp.sum(above.astype(F32), axis=(1, 2), keepdims=True)

    ri = lax.broadcasted_iota(I32, (rows, rows), 0)
    ci = lax.broadcasted_iota(I32, (rows, rows), 1)
    nb_shift = nb.bit_length() - 1
    assert nb == 1 << nb_shift
    same_expert = lax.shift_right_logical(ri, nb_shift) == lax.shift_right_logical(ci, nb_shift)
    seg_lower = jnp.logical_and(same_expert, ci < ri).astype(BF16)
    tri_upper = (lax.broadcasted_iota(I32, (ROUTE_BLOCK, ROUTE_BLOCK), 0)
                 <= lax.broadcasted_iota(I32, (ROUTE_BLOCK, ROUTE_BLOCK), 1)).astype(BF16)

    tied_f = tied.astype(F32).reshape(rows, ROUTE_BLOCK)
    tied_rank = _segment_count(tied_f, seg_lower, tri_upper)[0] - tied_f
    chosen = jnp.logical_or(above, jnp.logical_and(tied, tied_rank.reshape(n_e, nb, ROUTE_BLOCK) < need))
    chosen_f = chosen.astype(F32).reshape(rows, ROUTE_BLOCK)
    incl, before, total = _segment_count(chosen_f, seg_lower, tri_upper)
    slot_ref[...] = jnp.where(chosen_f > 0, incl - 1.0, -1.0).astype(I32)
    before_ref[...] = before.astype(I32)
    total_ref[...] = total.astype(I32)


def _route_select(aff3, cap):
    n_e, nb, _ = aff3.shape
    rows = n_e * nb
    table = jax.ShapeDtypeStruct((rows, ROUTE_BLOCK), I32)
    return pl.pallas_call(
        functools.partial(_route_select_kernel, cap=cap),
        out_shape=[table, table, table],
        compiler_params=_params(None, 48),
        name="route_select",
    )(aff3)


def _route_index_kernel(slot_ref, aff_ref, before_ref, ends_ref, totals_ref, idx_ref, gate_ref):
    cap = idx_ref.shape[1]
    nb = slot_ref.shape[0]
    s_col = lax.broadcasted_iota(I32, (cap, nb), 0)
    done = ends_ref[0] <= s_col
    blk = jnp.sum(done.astype(F32), axis=1, keepdims=True)
    blk_first = jnp.sum(jnp.where(done, totals_ref[0].astype(F32), 0.0), axis=1, keepdims=True)
    onehot = (lax.broadcasted_iota(I32, (cap, nb), 1) == blk.astype(I32)).astype(BF16)
    slot = slot_ref[...]
    local = jnp.where(slot >= 0, slot - before_ref[...] + 1, 0).astype(BF16)
    picked = _dot(onehot, local)
    target = s_col[:, :1].astype(F32) - blk_first + 1.0
    match = picked == target
    lane = lax.broadcasted_iota(I32, match.shape, 1).astype(F32)
    tok = blk * ROUTE_BLOCK + jnp.sum(jnp.where(match, lane, 0.0), axis=1, keepdims=True)
    idx_ref[0] = tok.astype(I32)
    a = aff_ref[...]
    hi = a.astype(BF16)
    r1 = a - hi.astype(F32)
    mid = r1.astype(BF16)
    lo = (r1 - mid.astype(F32)).astype(BF16)
    a_rows = _dot(onehot, hi) + _dot(onehot, mid) + _dot(onehot, lo)
    gate_ref[0] = jnp.sum(jnp.where(match, a_rows, 0.0), axis=1, keepdims=True)


def _route_index(slot, aff2, before, ends, totals, cap):
    rows = slot.shape[0]
    n_e = N_EXPERTS
    nb = rows // n_e
    blocks = pl.BlockSpec((nb, ROUTE_BLOCK), lambda e: (e, 0))
    row = pl.BlockSpec((1, 1, nb), lambda e: (e, 0, 0))
    out = pl.BlockSpec((1, cap, 1), lambda e: (e, 0, 0))
    return pl.pallas_call(
        _route_index_kernel,
        grid=(n_e,),
        in_specs=[blocks, blocks, blocks, row, row],
        out_specs=[out, out],
        out_shape=[jax.ShapeDtypeStruct((n_e, cap, 1), I32), jax.ShapeDtypeStruct((n_e, cap, 1), F32)],
        compiler_params=_params(("parallel",), 48),
        name="route_index",
    )(slot, aff2, before, ends, totals)


def _ffn_kernel(idx_ref, xn_hbm, gate_ref, wg_ref, wu_ref, wd_ref, ye_ref, stage_ref, xe_ref, acc_ref, sem):
    e = pl.program_id(0)
    f = pl.program_id(1)
    cap = xe_ref.shape[0]

    def start_gather(expert, buf):
        def start(group, c):
            for j in range(SUBLANES):
                tok = idx_ref[expert * cap + group * SUBLANES + j]
                src = xn_hbm.at[lax.shift_right_logical(tok, 3), pl.ds(jnp.bitwise_and(tok, SUBLANES - 1), 1), :]
                pltpu.make_async_copy(src, stage_ref.at[buf, group, pl.ds(j, 1), :], sem.at[buf]).start()
            return c

        lax.fori_loop(0, cap // SUBLANES, start, 0)

    @pl.when(f == 0)
    def _():
        @pl.when(e == 0)
        def _():
            start_gather(0, 0)

        buf = e % 2
        pltpu.make_async_copy(xn_hbm.at[pl.ds(0, cap // SUBLANES)], stage_ref.at[buf], sem.at[buf]).wait()
        xe_ref[...] = stage_ref[buf].reshape(cap, D_MODEL).astype(BF16)
        acc_ref[...] = jnp.zeros_like(acc_ref)

        @pl.when(e + 1 < pl.num_programs(0))
        def _():
            start_gather(e + 1, 1 - buf)

    xe = xe_ref[...]
    g = _dot(xe, wg_ref[0].astype(BF16))
    u = _dot(xe, wu_ref[0].astype(BF16))
    h = (_silu(g) * u).astype(BF16)
    acc_ref[...] += _dot(h, wd_ref[0].astype(BF16))

    @pl.when(f == pl.num_programs(1) - 1)
    def _():
        ye_ref[0] = (acc_ref[...] * gate_ref[0]).astype(BF16)


def _ffn(idx, xn, gates3, w_gate, w_up, w_down):
    e, cap = idx.shape
    tf = 256
    grid_spec = pltpu.PrefetchScalarGridSpec(
        num_scalar_prefetch=1,
        grid=(e, EXPERT_FF // tf),
        in_specs=[
            pl.BlockSpec(memory_space=pl.ANY),
            pl.BlockSpec((1, cap, 1), lambda ee, f, ix: (ee, 0, 0)),
            pl.BlockSpec((1, D_MODEL, tf), lambda ee, f, ix: (ee, 0, f)),
            pl.BlockSpec((1, D_MODEL, tf), lambda ee, f, ix: (ee, 0, f)),
            pl.BlockSpec((1, tf, D_MODEL), lambda ee, f, ix: (ee, f, 0)),
        ],
        out_specs=pl.BlockSpec((1, cap, D_MODEL), lambda ee, f, ix: (ee, 0, 0)),
        scratch_shapes=[
            pltpu.VMEM((2, cap // SUBLANES, SUBLANES, D_MODEL), F32),
            pltpu.VMEM((cap, D_MODEL), BF16),
            pltpu.VMEM((cap, D_MODEL), F32),
            pltpu.SemaphoreType.DMA((2,)),
        ],
    )
    return pl.pallas_call(
        _ffn_kernel,
        grid_spec=grid_spec,
        out_shape=jax.ShapeDtypeStruct((e, cap, D_MODEL), BF16),
        compiler_params=_params(("arbitrary", "arbitrary"), 56),
        name="moe_ffn",
    )(idx.reshape(-1), xn.reshape(-1, SUBLANES, D_MODEL), gates3, w_gate, w_up, w_down)


def _combine_kernel(lo_ref, cnt_ref, slot_ref, x1_ref, lo_row_ref, cnt_row_ref, lo_col_ref, cnt_col_ref, ye_hbm,
                    y_ref, stage_ref, sem):
    i = pl.program_id(0)
    nt = pl.num_programs(0)
    n_e = ye_hbm.shape[0]
    tt = slot_ref.shape[0]
    shift = BF16_ROWS.bit_length() - 1

    def plan(tile):
        entries = []
        base = jnp.int32(0)
        for e in range(n_e):
            lo = lo_ref[tile * n_e + e]
            cnt = cnt_ref[tile * n_e + e]
            first = lax.shift_right_logical(lo, shift)
            groups = jnp.where(cnt > 0, lax.shift_right_logical(lo + cnt + (BF16_ROWS - 1), shift) - first, 0)
            entries.append((first * BF16_ROWS, groups, base))
            base = base + groups * BF16_ROWS
        return entries, base

    def for_each_group(tile, buf, fn):
        entries, _ = plan(tile)
        for e, (a0, groups, base) in enumerate(entries):
            def body(j, c, e=e, a0=a0, base=base):
                src = ye_hbm.at[e, pl.ds(pl.multiple_of(a0 + j * BF16_ROWS, BF16_ROWS), BF16_ROWS), :]
                dst = stage_ref.at[buf, pl.ds(pl.multiple_of(base + j * BF16_ROWS, BF16_ROWS), BF16_ROWS), :]
                fn(pltpu.make_async_copy(src, dst, sem.at[buf]))
                return c

            lax.fori_loop(0, groups, body, 0)

    buf = i % 2

    @pl.when(i == 0)
    def _():
        stage_ref[...] = jnp.zeros_like(stage_ref)
        for_each_group(0, 0, lambda cp: cp.start())

    @pl.when(i + 1 < nt)
    def _():
        for_each_group(i + 1, 1 - buf, lambda cp: cp.start())

    for_each_group(i, buf, lambda cp: cp.wait())

    def staged(lo, cnt):
        first = lax.shift_right_logical(lo, shift)
        groups = jnp.where(cnt > 0, lax.shift_right_logical(lo + cnt + (BF16_ROWS - 1), shift) - first, 0)
        return first * BF16_ROWS, groups * BF16_ROWS

    a0_row, rows_row = staged(lo_row_ref[0], cnt_row_ref[0])
    _, rows_col = staged(lo_col_ref[0], cnt_col_ref[0])
    ei = lax.broadcasted_iota(I32, (n_e, n_e), 0)
    ej = lax.broadcasted_iota(I32, (n_e, n_e), 1)
    base_row = _dot(rows_row.astype(F32).astype(BF16), (ei < ej).astype(BF16)).astype(I32)
    base_col = _dot((ej < ei).astype(BF16),
                    jnp.broadcast_to(rows_col, (n_e, LANE)).astype(F32).astype(BF16))[:, :1].astype(I32)
    slot = slot_ref[...]
    rel = jnp.where(slot >= 0, slot - a0_row + base_row, -1)
    rel_hi = lax.shift_right_arithmetic(rel, 6).astype(F32).astype(BF16)
    rel_lo = jnp.bitwise_and(rel, 63).astype(F32).astype(BF16)
    _, total = plan(i)
    y_ref[...] = x1_ref[...]

    def k_block(kb, c):
        first = kb * COMBINE_K
        col_e = lax.broadcasted_iota(I32, (n_e, COMBINE_K), 1) + first
        owner = jnp.logical_and(col_e >= base_col, col_e < base_col + rows_col).astype(BF16)
        want = 64.0 * _dot(rel_hi, owner) + _dot(rel_lo, owner)
        col_t = (lax.broadcasted_iota(I32, (tt, COMBINE_K), 1) + first).astype(F32)
        rows = pl.ds(pl.multiple_of(first, COMBINE_K), COMBINE_K)
        y_ref[...] += _dot((want == col_t).astype(BF16), stage_ref[buf, rows, :])
        return c

    lax.fori_loop(0, lax.shift_right_logical(total + (COMBINE_K - 1), COMBINE_K.bit_length() - 1), k_block, 0)


def _combine(lo_flat, cnt_flat, slot, x1, ye):
    t = x1.shape[0]
    n_e = ye.shape[0]
    tt = ROUTE_BLOCK
    stage_rows = n_e * (tt + BF16_ROWS)
    assert stage_rows % COMBINE_K == 0
    nt = t // tt
    lo2 = lo_flat.reshape(nt, n_e)
    cnt2 = cnt_flat.reshape(nt, n_e)
    row_spec = pl.BlockSpec((1, 1, n_e), lambda i, lo, cnt: (i, 0, 0))
    col_spec = pl.BlockSpec((1, n_e, 1), lambda i, lo, cnt: (i, 0, 0))
    grid_spec = pltpu.PrefetchScalarGridSpec(
        num_scalar_prefetch=2,
        grid=(nt,),
        in_specs=[
            pl.BlockSpec((tt, n_e), lambda i, lo, cnt: (i, 0)),
            pl.BlockSpec((tt, D_MODEL), lambda i, lo, cnt: (i, 0)),
            row_spec, row_spec, col_spec, col_spec,
            pl.BlockSpec(memory_space=pl.ANY),
        ],
        out_specs=pl.BlockSpec((tt, D_MODEL), lambda i, lo, cnt: (i, 0)),
        scratch_shapes=[
            pltpu.VMEM((2, stage_rows, D_MODEL), BF16),
            pltpu.SemaphoreType.DMA((2,)),
        ],
    )
    return pl.pallas_call(
        _combine_kernel,
        grid_spec=grid_spec,
        out_shape=jax.ShapeDtypeStruct((t, D_MODEL), F32),
        compiler_params=_params(("arbitrary",), 48),
        name="moe_combine",
    )(lo_flat, cnt_flat, slot, x1, lo2[:, None, :], cnt2[:, None, :], lo2[:, :, None], cnt2[:, :, None], ye)


def _rope_tables(s):
    half = QK_ROPE // 2
    inv_freq = ROPE_THETA ** (-jnp.arange(half, dtype=F32) / half)
    ang = jnp.arange(s, dtype=F32)[:, None] * inv_freq[None, :]
    cos = jnp.cos(ang)
    sin = jnp.sin(ang)
    zeros = jnp.zeros((s, LANE - QK_ROPE), F32)
    return jnp.concatenate([cos, cos, zeros], axis=1), jnp.concatenate([-sin, sin, zeros], axis=1)


def _prepare_weights(norm1_g, w_in, q_a_norm_g, w_q_b, kv_a_norm_g, w_kv_b, q_norm_g, k_norm_g, conv_w,
                     a_log_fwd, a_log_bwd, dt_bias_fwd, dt_bias_bwd, gdn_norm_g, w_o, norm2_g, w_router):
    o_kv = Q_LORA
    o_kr = o_kv + KV_LORA
    o_qkv = o_kr + QK_ROPE
    o_z = o_qkv + QKV_GDN
    o_g = o_z + GDN_WIDTH
    wa = jnp.concatenate([w_in[:, o_z:o_g], w_in[:, :o_kr]], axis=1).astype(BF16)
    wb = jnp.concatenate(
        [w_in[:, o_qkv:o_z], w_in[:, o_kr:o_qkv], w_in[:, o_g:], jnp.zeros((D_MODEL, LANE - QK_ROPE - 4 * GDN_HEADS), F32)],
        axis=1).astype(BF16)
    wq = jnp.pad(w_q_b.reshape(Q_LORA, MLA_HEADS, QK_HEAD), ((0, 0), (0, 0), (0, QK_PAD - QK_HEAD)))
    wq = wq.reshape(Q_LORA, MLA_HEADS * QK_PAD).astype(BF16)
    pad_g = lambda g: jnp.pad(g, (0, QK_PAD - QK_HEAD)).reshape(1, QK_PAD)
    alog = jnp.concatenate([a_log_fwd, a_log_bwd]).reshape(2 * GDN_HEADS, 1, 1)
    dtb = jnp.concatenate([dt_bias_fwd, dt_bias_bwd]).reshape(2 * GDN_HEADS, 1, 1)
    return dict(
        g1=norm1_g.reshape(1, D_MODEL), wa=wa, wb=wb,
        gqa=q_a_norm_g.reshape(1, Q_LORA), gkva=kv_a_norm_g.reshape(1, KV_LORA),
        wq=wq, wkv=w_kv_b.astype(BF16), gq=pad_g(q_norm_g), gk=pad_g(k_norm_g),
        conv_w=conv_w,
        alog=jnp.broadcast_to(alog, (2 * GDN_HEADS, 1, LANE)), dtb=jnp.broadcast_to(dtb, (2 * GDN_HEADS, 1, LANE)),
        gn=gdn_norm_g.reshape(1, GDN_DV), wo=w_o.astype(BF16), g2=norm2_g.reshape(1, D_MODEL), wr=w_router,
    )


def _route(aff, cap):
    t, n_e = aff.shape
    nb = t // ROUTE_BLOCK
    aff3 = aff.T.reshape(n_e, nb, ROUTE_BLOCK)
    slot, before, total = _route_select(aff3, cap)
    before_b = before[:, 0].reshape(n_e, nb)
    total_b = total[:, 0].reshape(n_e, nb)
    idx, gates = _route_index(slot, aff3.reshape(n_e * nb, ROUTE_BLOCK), before,
                              (before_b + total_b).reshape(n_e, 1, nb), total_b.reshape(n_e, 1, nb), cap)
    slot_by_token = slot.reshape(n_e, t).T
    return idx.reshape(n_e, cap), gates, slot_by_token, before_b.T.reshape(-1), total_b.T.reshape(-1)


def _encoder_layer(x, p, w_gate, w_up, w_down):
    bt, s, _ = x.shape
    t = bt * s
    x2d = x.reshape(t, D_MODEL)
    oa, ob = _in_proj(x2d, p["g1"], p["wa"], p["wb"])

    cos, sin = _rope_tables(s)
    q, k, v = _mla_prep(oa, ob, cos, sin, p["gqa"], p["gkva"], p["wq"], p["wkv"], p["gq"], p["gk"], bt, s)
    mla_out = _attention(q, k, v)

    ob3 = ob.reshape(bt, s, NB)
    qkv_h = _conv(ob3, p["conv_w"])
    g_raw = ob3[:, :, QKV_GDN + QK_ROPE:QKV_GDN + QK_ROPE + 4 * GDN_HEADS]
    g_t = jnp.swapaxes(g_raw, 1, 2).reshape(bt, 4 * GDN_HEADS, s // LANE, LANE)
    g_out = _gates(g_t, p["alog"], p["dtb"]).reshape(bt, 4, GDN_HEADS, s)
    g_rows = g_out[:, :2].reshape(bt, 2 * GDN_HEADS, s // CHUNK, CHUNK)
    g_cols = jnp.transpose(g_out, (0, 2, 3, 1))
    o_fwd, o_bwd = _gdn_scan(_gdn_prep(qkv_h, g_rows, g_cols), g_rows)

    x1, xn, aff = _out_proj(x2d, mla_out.reshape(t, MLA_WIDTH), o_fwd.reshape(t, GDN_WIDTH),
                            o_bwd.reshape(t, GDN_WIDTH), oa, p["gn"], p["wo"], p["g2"], p["wr"])

    cap = max(1, CAPACITY_FACTOR * t // N_EXPERTS)
    idx, gates, slot, lo, cnt = _route(aff, cap)
    ye = _ffn(idx, xn, gates, w_gate, w_up, w_down)
    y = _combine(lo, cnt, slot, x1, ye)
    return y.reshape(bt, s, D_MODEL)


def kernel(x_prompt, x_sample, norm1_g, w_in, q_a_norm_g, w_q_b, kv_a_norm_g, w_kv_b, q_norm_g, k_norm_g, conv_w,
           a_log_fwd, a_log_bwd, dt_bias_fwd, dt_bias_bwd, gdn_norm_g, w_o, norm2_g, w_router, w_gate, w_up, w_down):
    y_prompt = x_prompt
    y_sample = x_sample
    for l in range(norm1_g.shape[0]):
        p = _prepare_weights(norm1_g[l], w_in[l], q_a_norm_g[l], w_q_b[l], kv_a_norm_g[l], w_kv_b[l], q_norm_g[l],
                             k_norm_g[l], conv_w[l], a_log_fwd[l], a_log_bwd[l], dt_bias_fwd[l], dt_bias_bwd[l],
                             gdn_norm_g[l], w_o[l], norm2_g[l], w_router[l])
        y_prompt = _encoder_layer(y_prompt, p, w_gate[l], w_up[l], w_down[l])
        y_sample = _encoder_layer(y_sample, p, w_gate[l], w_up[l], w_down[l])
    return (y_prompt, y_sample)
```

```python
import functools
import math

import jax
import jax.numpy as jnp
from jax import lax
from jax.experimental import pallas as pl
from jax.experimental.pallas import tpu as pltpu

F32 = jnp.float32
BF16 = jnp.bfloat16
I32 = jnp.int32

D_MODEL = 2048
MLA_HEADS = 8
Q_LORA = 512
KV_LORA = 256
QK_NOPE = 128
QK_ROPE = 64
QK_HEAD = QK_NOPE + QK_ROPE
QK_PAD = 256
V_HEAD = 128
V_PAD = 256
ROPE_THETA = 10000.0
GDN_HEADS = 8
GDN_DK = 128
GDN_DV = 128
QKV_GDN = 2 * GDN_HEADS * GDN_DK + GDN_HEADS * GDN_DV
GDN_WIDTH = GDN_HEADS * GDN_DV
MLA_WIDTH = MLA_HEADS * V_HEAD
CONV_W = 5
CHUNK = 64
N_EXPERTS = 16
EXPERT_FF = 1024
CAPACITY_FACTOR = 2
EPS = 1e-6

LANE = 128
SUBLANES = 8
BF16_ROWS = 16
MIB = 1024 * 1024

NA = GDN_WIDTH + Q_LORA + KV_LORA
NB = QKV_GDN + LANE
QLAT_COL_BLOCK = GDN_WIDTH // Q_LORA
KVLAT_COL_BLOCK = (GDN_WIDTH + Q_LORA) // KV_LORA
MISC_COL_BLOCK = QKV_GDN // LANE
PREP_CHUNKS = 8
ATTN_SUB = 256
ROUTE_BLOCK = LANE
COMBINE_K = 256
EXP_SPAN = 126
MANTISSA_STEPS = 40

NN = (((1,), (0,)), ((), ()))
NT = (((1,), (1,)), ((), ()))
TN = (((0,), (0,)), ((), ()))


def _params(semantics, vmem_mib):
    return pltpu.CompilerParams(dimension_semantics=semantics, vmem_limit_bytes=vmem_mib * MIB)


def _resident(shape):
    nd = len(shape)
    return pl.BlockSpec(shape, lambda *_: (0,) * nd, pipeline_mode=pl.Buffered(1))


def _dot(a, b, dims=NN):
    return lax.dot_general(a, b, dims, preferred_element_type=F32)


def _dotb(a, b, dims=NN):
    return lax.dot_general(a.astype(BF16), b.astype(BF16), dims, preferred_element_type=F32)


def _split(x):
    hi = x.astype(BF16)
    lo = (x - hi.astype(F32)).astype(BF16)
    return hi, lo


def _dot3(a, b, dims=NN):
    ah, al = _split(a)
    bh, bl = _split(b)
    return _dot(ah, bh, dims) + (_dot(ah, bl, dims) + _dot(al, bh, dims))


def _silu(x):
    return x * jax.nn.sigmoid(x)


def _in_proj_kernel(x_ref, g_ref, wz_ref, wlat_ref, wqkv_ref, wmisc_ref, oa_ref, ob_ref):
    x = x_ref[...]
    ms = jnp.mean(x * x, axis=-1, keepdims=True)
    h = (x * lax.rsqrt(ms + EPS) * g_ref[...]).astype(BF16)
    oa_ref[:, :GDN_WIDTH] = _dot(h, wz_ref[...]).astype(BF16)
    oa_ref[:, GDN_WIDTH:] = _dot(h, wlat_ref[...]).astype(BF16)
    ob_ref[:, :QKV_GDN] = _dot(h, wqkv_ref[...])
    ob_ref[:, QKV_GDN:] = _dot(h, wmisc_ref[...])


def _in_proj(x2d, g, wz, wlat, wqkv, wmisc):
    t = x2d.shape[0]
    tm = min(512, t)
    return pl.pallas_call(
        _in_proj_kernel,
        grid=(t // tm,),
        in_specs=[
            pl.BlockSpec((tm, D_MODEL), lambda i: (i, 0)),
            _resident((1, D_MODEL)),
            _resident(wz.shape),
            _resident(wlat.shape),
            _resident(wqkv.shape),
            _resident(wmisc.shape),
        ],
        out_specs=[
            pl.BlockSpec((tm, NA), lambda i: (i, 0)),
            pl.BlockSpec((tm, NB), lambda i: (i, 0)),
        ],
        out_shape=[
            jax.ShapeDtypeStruct((t, NA), BF16),
            jax.ShapeDtypeStruct((t, NB), F32),
        ],
        compiler_params=_params(("parallel",), 56),
        name="in_proj",
    )(x2d, g, wz, wlat, wqkv, wmisc)


def _mla_prep_kernel(ql_ref, kvl_ref, misc_ref, cos_ref, sin_ref, gqa_ref, gkva_ref, wq_ref, wkv_ref,
                     gq_ref, gk_ref, q_ref, k_ref, v_ref):
    def norm(x, g):
        return x * lax.rsqrt(jnp.mean(x * x, axis=-1, keepdims=True) + EPS) * g

    qn = norm(ql_ref[...].astype(F32), gqa_ref[...]).astype(BF16)
    kvn = norm(kvl_ref[...].astype(F32), gkva_ref[...]).astype(BF16)
    q = _dot(qn, wq_ref[...])
    kv = _dot(kvn, wkv_ref[...])

    cos = cos_ref[...]
    sin = sin_ref[...]
    lane = lax.broadcasted_iota(I32, cos.shape, 1)

    def rope(t):
        rot = jnp.where(lane < QK_ROPE // 2, pltpu.roll(t, LANE - QK_ROPE // 2, 1), pltpu.roll(t, QK_ROPE // 2, 1))
        return t * cos + rot * sin

    gq = gq_ref[...]
    gk = gk_ref[...]
    kr = jnp.where(lane < QK_ROPE, misc_ref[...], 0.0)
    kr_ssq = jnp.sum(kr * kr, axis=-1, keepdims=True)
    kr_rot = rope(kr * gk[:, QK_NOPE:])
    q_scale = QK_HEAD ** -0.5
    ones_col = jnp.where(lane == 0, 1.0, 0.0).astype(BF16)
    for h in range(MLA_HEADS):
        qh = q[:, h * QK_PAD:(h + 1) * QK_PAD]
        sc = lax.rsqrt(jnp.sum(qh * qh, axis=-1, keepdims=True) * (1.0 / QK_HEAD) + EPS) * q_scale
        q_ref[0, h, :, :QK_NOPE] = (qh[:, :QK_NOPE] * gq[:, :QK_NOPE] * sc).astype(BF16)
        q_ref[0, h, :, QK_NOPE:] = (rope(qh[:, QK_NOPE:] * gq[:, QK_NOPE:]) * sc).astype(BF16)
        kn = kv[:, h * 256:h * 256 + QK_NOPE]
        sk = lax.rsqrt((jnp.sum(kn * kn, axis=-1, keepdims=True) + kr_ssq) * (1.0 / QK_HEAD) + EPS)
        k_ref[0, h, :, :QK_NOPE] = (kn * gk[:, :QK_NOPE] * sk).astype(BF16)
        k_ref[0, h, :, QK_NOPE:] = (kr_rot * sk).astype(BF16)
        v_ref[0, h, :, :V_HEAD] = kv[:, h * 256 + QK_NOPE:(h + 1) * 256].astype(BF16)
        v_ref[0, h, :, V_HEAD:] = ones_col


def _mla_prep(oa, ob, cos, sin, gqa, gkva, wq, wkv, gq, gk, bt, s):
    tm = min(512, s)
    nt = s // tm
    return pl.pallas_call(
        _mla_prep_kernel,
        grid=(bt, nt),
        in_specs=[
            pl.BlockSpec((tm, Q_LORA), lambda b, i: (b * nt + i, QLAT_COL_BLOCK)),
            pl.BlockSpec((tm, KV_LORA), lambda b, i: (b * nt + i, KVLAT_COL_BLOCK)),
            pl.BlockSpec((tm, LANE), lambda b, i: (b * nt + i, MISC_COL_BLOCK)),
            pl.BlockSpec((tm, LANE), lambda b, i: (i, 0)),
            pl.BlockSpec((tm, LANE), lambda b, i: (i, 0)),
            _resident(gqa.shape),
            _resident(gkva.shape),
            _resident(wq.shape),
            _resident(wkv.shape),
            _resident(gq.shape),
            _resident(gk.shape),
        ],
        out_specs=[
            pl.BlockSpec((1, MLA_HEADS, tm, QK_PAD), lambda b, i: (b, 0, i, 0)),
            pl.BlockSpec((1, MLA_HEADS, tm, QK_PAD), lambda b, i: (b, 0, i, 0)),
            pl.BlockSpec((1, MLA_HEADS, tm, V_PAD), lambda b, i: (b, 0, i, 0)),
        ],
        out_shape=[
            jax.ShapeDtypeStruct((bt, MLA_HEADS, s, QK_PAD), BF16),
            jax.ShapeDtypeStruct((bt, MLA_HEADS, s, QK_PAD), BF16),
            jax.ShapeDtypeStruct((bt, MLA_HEADS, s, V_PAD), BF16),
        ],
        compiler_params=_params(("parallel", "parallel"), 48),
        name="mla_prep",
    )(oa, oa, ob, cos, sin, gqa, gkva, wq, wkv, gq, gk)


def _attn_kernel(q_ref, k_ref, v_ref, o_ref):
    k = k_ref[0, 0]
    v = v_ref[0, 0]
    parts = [slice(r, r + ATTN_SUB) for r in range(0, q_ref.shape[2], ATTN_SUB)]
    scores = [_dot(q_ref[0, 0, rows, :], k, NT) for rows in parts]
    for rows, s in zip(parts, scores):
        m = jnp.max(s, axis=-1, keepdims=True)
        p = jnp.exp((s - m).astype(BF16))
        o = _dot(p, v)
        o_ref[0, rows, :] = (o[:, :V_HEAD] / o[:, V_HEAD:V_HEAD + 1]).astype(BF16)


def _attention(q, k, v):
    bt, h, s, _ = q.shape
    tq = min(4 * ATTN_SUB, s)
    return pl.pallas_call(
        _attn_kernel,
        grid=(bt, h, s // tq),
        in_specs=[
            pl.BlockSpec((1, 1, tq, QK_PAD), lambda b, hh, i: (b, hh, i, 0)),
            pl.BlockSpec((1, 1, s, QK_PAD), lambda b, hh, i: (b, hh, 0, 0)),
            pl.BlockSpec((1, 1, s, V_PAD), lambda b, hh, i: (b, hh, 0, 0)),
        ],
        out_specs=pl.BlockSpec((1, tq, V_HEAD), lambda b, hh, i: (b, i, hh)),
        out_shape=jax.ShapeDtypeStruct((bt, s, MLA_WIDTH), BF16),
        compiler_params=_params(("parallel", "parallel", "parallel"), 48),
        name="attention",
    )(q, k, v)


def _conv_kernel(x_ref, w_ref, o_ref, pad_ref):
    c = pl.program_id(1)
    x = x_ref[0]
    s = x.shape[0]
    w = w_ref[...]
    pad = CONV_W // 2
    halo = jnp.zeros((SUBLANES, LANE), F32)
    pad_ref[:SUBLANES, :] = halo
    pad_ref[SUBLANES + s:, :] = halo
    pad_ref[SUBLANES:SUBLANES + s, :] = x
    acc = x * w[pad:pad + 1]
    for d in range(-pad, pad + 1):
        if d != 0:
            acc = acc + pad_ref[SUBLANES + d:SUBLANES + d + s, :] * w[pad + d:pad + d + 1]
    y = _silu(acc)
    inv = lax.rsqrt(jnp.sum(y * y, axis=-1, keepdims=True) + EPS)
    is_q = c < GDN_HEADS
    is_qk = c < 2 * GDN_HEADS
    o_ref[0, 0] = y * (jnp.where(is_qk, inv, 1.0) * jnp.where(is_q, GDN_DK ** -0.5, 1.0))


def _conv(ob3, conv_w):
    bt, s, _ = ob3.shape
    nblk = QKV_GDN // LANE
    return pl.pallas_call(
        _conv_kernel,
        grid=(bt, nblk),
        in_specs=[
            pl.BlockSpec((1, s, LANE), lambda b, c: (b, 0, c)),
            pl.BlockSpec((CONV_W, LANE), lambda b, c: (0, c)),
        ],
        out_specs=pl.BlockSpec((1, 1, s, LANE), lambda b, c: (b, c, 0, 0)),
        out_shape=jax.ShapeDtypeStruct((bt, nblk, s, LANE), F32),
        scratch_shapes=[pltpu.VMEM((s + 2 * SUBLANES, LANE), F32)],
        compiler_params=_params(("parallel", "parallel"), 48),
        name="gdn_conv",
    )(ob3, conv_w)


def _gates_kernel(x_ref, alog_ref, dtb_ref, o_ref):
    x = x_ref[0]
    nh = GDN_HEADS
    a = x[:2 * nh] + dtb_ref[...]
    softplus = jnp.maximum(a, 0.0) + jnp.log1p(jnp.exp(-jnp.abs(a)))
    g = -jnp.exp(alog_ref[...]) * softplus
    beta = jax.nn.sigmoid(x[2 * nh:])
    gf = g[:nh]
    gb = g[nh:]
    pos = lax.broadcasted_iota(I32, gf.shape, 2) % CHUNK
    sh = 1
    while sh < CHUNK:
        gf = gf + jnp.where(pos >= sh, pltpu.roll(gf, sh, 2), 0.0)
        gb = gb + jnp.where(pos < CHUNK - sh, pltpu.roll(gb, LANE - sh, 2), 0.0)
        sh *= 2
    o_ref[0, :nh] = gf
    o_ref[0, nh:2 * nh] = gb
    o_ref[0, 2 * nh:] = beta


def _gates(gt, alog, dtb):
    bt, r, sl, _ = gt.shape
    return pl.pallas_call(
        _gates_kernel,
        grid=(bt,),
        in_specs=[
            pl.BlockSpec((1, r, sl, LANE), lambda b: (b, 0, 0, 0)),
            _resident(alog.shape),
            _resident(dtb.shape),
        ],
        out_specs=pl.BlockSpec((1, r, sl, LANE), lambda b: (b, 0, 0, 0)),
        out_shape=jax.ShapeDtypeStruct(gt.shape, F32),
        compiler_params=_params(("parallel",), 32),
        name="gdn_gates",
    )(gt, alog, dtb)


def _unit_tri_inverses(lmats, eye):
    ps = [eye - l for l in lmats]
    ms = [_dotb(l, l) for l in lmats]
    n = 2
    while n < CHUNK // 2:
        ps = [p + _dotb(p, m) for p, m in zip(ps, ms)]
        ms = [_dotb(m, m) for m in ms]
        n *= 2
    return [p + _dotb(p, m) for p, m in zip(ps, ms)]


def _gdn_prep_kernel(q_ref, k_ref, v_ref, grf_ref, grb_ref, gc_ref,
                     wqf_ref, uf_ref, kdf_ref, inf_ref, wqb_ref, ub_ref, kdb_ref, inb_ref):
    ri = lax.broadcasted_iota(I32, (CHUNK, CHUNK), 0)
    ci = lax.broadcasted_iota(I32, (CHUNK, CHUNK), 1)
    eye = (ri == ci).astype(F32)
    dirs = (
        (0, grf_ref, ri >= ci, ri > ci, CHUNK - 1, (wqf_ref, uf_ref, kdf_ref, inf_ref)),
        (1, grb_ref, ri <= ci, ri < ci, 0, (wqb_ref, ub_ref, kdb_ref, inb_ref)),
    )
    lmats = []
    work = []
    for c in range(PREP_CHUNKS):
        rows = slice(c * CHUNK, (c + 1) * CHUNK)
        q = q_ref[0, 0, rows, :]
        k = k_ref[0, 0, rows, :]
        v = v_ref[0, 0, rows, :]
        g4 = gc_ref[0, 0, rows, :]
        kk = _dotb(k, k, NT)
        qk = _dotb(q, k, NT)
        for col, grow_ref, incl, strict, last_row, outs in dirs:
            gcol = g4[:, col:col + 1]
            beta = g4[:, 2 + col:3 + col]
            decay = jnp.exp(jnp.where(incl, gcol - grow_ref[0, 0, c:c + 1, :], -jnp.inf))
            lmats.append(jnp.where(strict, beta * kk * decay, 0.0))
            e_gc = jnp.exp(gcol)
            rhs = jnp.concatenate([v * beta, k * (beta * e_gc)], axis=-1)
            k_dec = k * jnp.exp(g4[last_row:last_row + 1, col:col + 1] - gcol)
            intra = jnp.where(incl, qk * decay, 0.0)
            work.append((c, rows, rhs, q * e_gc, k_dec, intra, outs))
    tinvs = _unit_tri_inverses(lmats, eye)
    for tinv, (c, rows, rhs, q_dec, k_dec, intra, (wq_ref, u_ref, kd_ref, in_ref)) in zip(tinvs, work):
        sol = _dotb(tinv, rhs)
        u_ref[0, 0, rows, :] = sol[:, :GDN_DV].astype(BF16)
        wq_ref[0, 0, c, :CHUNK, :] = sol[:, GDN_DV:].astype(BF16)
        wq_ref[0, 0, c, CHUNK:, :] = q_dec.astype(BF16)
        kd_ref[0, 0, rows, :] = k_dec.astype(BF16)
        in_ref[0, 0, rows, :] = intra.astype(BF16)


def _gdn_prep(qkv_h, g_rows, g_cols):
    bt, _, s, _ = qkv_h.shape
    nh = GDN_HEADS
    rb = PREP_CHUNKS * CHUNK
    n_chunks = s // CHUNK
    head = lambda off: (lambda b, h, i: (b, off + h, i, 0))
    seq_spec = lambda w: pl.BlockSpec((1, 1, rb, w), head(0))
    wq_spec = pl.BlockSpec((1, 1, PREP_CHUNKS, 2 * CHUNK, GDN_DK), lambda b, h, i: (b, h, i, 0, 0))
    out_specs = [wq_spec, seq_spec(GDN_DV), seq_spec(GDN_DK), seq_spec(CHUNK)]
    out_shape = [
        jax.ShapeDtypeStruct((bt, nh, n_chunks, 2 * CHUNK, GDN_DK), BF16),
        jax.ShapeDtypeStruct((bt, nh, s, GDN_DV), BF16),
        jax.ShapeDtypeStruct((bt, nh, s, GDN_DK), BF16),
        jax.ShapeDtypeStruct((bt, nh, s, CHUNK), BF16),
    ]
    return pl.pallas_call(
        _gdn_prep_kernel,
        grid=(bt, nh, s // rb),
        in_specs=[
            pl.BlockSpec((1, 1, rb, GDN_DK), head(0)),
            pl.BlockSpec((1, 1, rb, GDN_DK), head(nh)),
            pl.BlockSpec((1, 1, rb, GDN_DV), head(2 * nh)),
            pl.BlockSpec((1, 1, PREP_CHUNKS, CHUNK), head(0)),
            pl.BlockSpec((1, 1, PREP_CHUNKS, CHUNK), head(nh)),
            pl.BlockSpec((1, 1, rb, 4), head(0)),
        ],
        out_specs=out_specs + out_specs,
        out_shape=out_shape + out_shape,
        compiler_params=_params(("parallel", "parallel", "parallel"), 48),
        name="gdn_prep",
    )(qkv_h, qkv_h, qkv_h, g_rows, g_rows, g_cols)


def _gdn_scan_kernel(wqf_ref, uf_ref, kdf_ref, inf_ref, grf_ref, wqb_ref, ub_ref, kdb_ref, inb_ref, grb_ref,
                     of_ref, ob_ref, state_ref):
    @pl.when(pl.program_id(1) == 0)
    def _():
        state_ref[...] = jnp.zeros_like(state_ref)

    def step(c, carry):
        dirs = (
            (0, c, wqf_ref, uf_ref, kdf_ref, inf_ref, grf_ref, CHUNK - 1, of_ref),
            (1, PREP_CHUNKS - 1 - c, wqb_ref, ub_ref, kdb_ref, inb_ref, grb_ref, 0, ob_ref),
        )
        chains = [(h,) + d for h in range(GDN_HEADS) for d in dirs]
        states = [state_ref[d, h] for h, d, *_ in chains]
        rs = [_dot(wq_ref[0, h, cc], s.astype(BF16)) for (h, _, cc, wq_ref, *_), s in zip(chains, states)]
        for (h, d, cc, _, u_ref, kd_ref, in_ref, g_ref, last_lane, o_ref), s, r in zip(chains, states, rs):
            rows = pl.ds(pl.multiple_of(cc * CHUNK, CHUNK), CHUNK)
            v_new = (u_ref[0, h, rows, :].astype(F32) - r[:CHUNK]).astype(BF16)
            o = r[CHUNK:] + _dot(in_ref[0, h, rows, :], v_new)
            o_ref[0, rows, h * GDN_DV:(h + 1) * GDN_DV] = o.astype(BF16)
            g_last = g_ref[0, d * GDN_HEADS + h, pl.ds(cc, 1), :][:, last_lane:last_lane + 1]
            state_ref[d, h] = s * jnp.exp(g_last) + _dot(kd_ref[0, h, rows, :], v_new, TN)
        return carry

    lax.fori_loop(0, PREP_CHUNKS, step, 0)


def _gdn_scan(prep, g_rows):
    wqf, uf, kdf, inf_, wqb, ub, kdb, inb = prep
    bt, nh, s, _ = uf.shape
    rb = PREP_CHUNKS * CHUNK
    nblk = s // rb
    fwd = lambda b, i: (b, 0, i, 0)
    bwd = lambda b, i: (b, 0, nblk - 1 - i, 0)

    def specs(idx):
        idx5 = lambda b, i: idx(b, i) + (0,)
        return [
            pl.BlockSpec((1, nh, PREP_CHUNKS, 2 * CHUNK, GDN_DK), idx5),
            pl.BlockSpec((1, nh, rb, GDN_DV), idx),
            pl.BlockSpec((1, nh, rb, GDN_DK), idx),
            pl.BlockSpec((1, nh, rb, CHUNK), idx),
            pl.BlockSpec((1, 2 * nh, PREP_CHUNKS, CHUNK), idx),
        ]

    out = jax.ShapeDtypeStruct((bt, s, GDN_WIDTH), BF16)
    return pl.pallas_call(
        _gdn_scan_kernel,
        grid=(bt, nblk),
        in_specs=specs(fwd) + specs(bwd),
        out_specs=[
            pl.BlockSpec((1, rb, GDN_WIDTH), lambda b, i: (b, i, 0)),
            pl.BlockSpec((1, rb, GDN_WIDTH), lambda b, i: (b, nblk - 1 - i, 0)),
        ],
        out_shape=[out, out],
        scratch_shapes=[pltpu.VMEM((2, nh, GDN_DK, GDN_DV), F32)],
        compiler_params=_params(("parallel", "arbitrary"), 48),
        name="gdn_scan",
    )(wqf, uf, kdf, inf_, g_rows, wqb, ub, kdb, inb, g_rows)


def _out_proj_kernel(x_ref, mla_ref, of_ref, ob_ref, z_ref, gn_ref, wo_ref, g_ref, wr_ref, x1_ref, xn_ref, aff_ref,
                     gdn_ref):
    for h in range(GDN_HEADS):
        cols = slice(h * GDN_DV, (h + 1) * GDN_DV)
        o = of_ref[:, cols].astype(F32) + ob_ref[:, cols].astype(F32)
        on = o * lax.rsqrt(jnp.mean(o * o, axis=-1, keepdims=True) + EPS) * gn_ref[...]
        gdn_ref[:, cols] = (on * _silu(z_ref[:, cols].astype(F32))).astype(BF16)
    y = x_ref[...] + _dot(mla_ref[...], wo_ref[:MLA_WIDTH, :]) + _dot(gdn_ref[...], wo_ref[MLA_WIDTH:, :])
    x1_ref[...] = y
    xn = y * lax.rsqrt(jnp.mean(y * y, axis=-1, keepdims=True) + EPS) * g_ref[...]
    xn_ref[...] = xn
    logits = _dot3(xn, wr_ref[...])
    e = jnp.exp(logits - jnp.max(logits, axis=-1, keepdims=True))
    aff_ref[...] = e / jnp.sum(e, axis=-1, keepdims=True)


def _out_proj(x2d, mla, o_fwd, o_bwd, oa, gn, wo, g2, wr):
    t = x2d.shape[0]
    tm = min(256, t)
    rows = lambda w: pl.BlockSpec((tm, w), lambda i: (i, 0))
    return pl.pallas_call(
        _out_proj_kernel,
        grid=(t // tm,),
        in_specs=[rows(D_MODEL), rows(MLA_WIDTH), rows(GDN_WIDTH), rows(GDN_WIDTH), rows(GDN_WIDTH),
                  _resident(gn.shape), _resident(wo.shape), _resident(g2.shape), _resident(wr.shape)],
        out_specs=[rows(D_MODEL), rows(D_MODEL), rows(N_EXPERTS)],
        out_shape=[
            jax.ShapeDtypeStruct((t, D_MODEL), F32),
            jax.ShapeDtypeStruct((t, D_MODEL), F32),
            jax.ShapeDtypeStruct((t, N_EXPERTS), F32),
        ],
        scratch_shapes=[pltpu.VMEM((tm, GDN_WIDTH), BF16)],
        compiler_params=_params(("parallel",), 48),
        name="out_proj",
    )(x2d, mla, o_fwd, o_bwd, oa, gn, wo, g2, wr)


def _segment_count(flags, seg_lower, tri_upper):
    within = _dot(flags.astype(BF16), tri_upper)
    total = jnp.broadcast_to(within[:, ROUTE_BLOCK - 1:], within.shape)
    before = _dot(seg_lower, total.astype(BF16))
    return within + before, before, total


def _route_select_kernel(aff_ref, slot_ref, before_ref, total_ref, *, cap):
    n_e, nb, _ = aff_ref.shape
    rows = n_e * nb
    aff = aff_ref[...]

    def count_ge(t):
        return jnp.sum((aff >= t).astype(F32), axis=(1, 2), keepdims=True)

    e_hi = jnp.full((n_e, 1, 1), -1.0, F32)
    e_lo = jnp.full((n_e, 1, 1), float(EXP_SPAN), F32)
    for _ in range(EXP_SPAN.bit_length()):
        e_mid = jnp.floor((e_hi + e_lo) * 0.5)
        enough = count_ge(jnp.exp2(-e_mid)) >= cap
        e_lo = jnp.where(enough, e_mid, e_lo)
        e_hi = jnp.where(enough, e_hi, e_mid)
    lo = jnp.where(e_lo >= EXP_SPAN, 0.0, jnp.exp2(-e_lo))
    hi = jnp.exp2(-e_hi)
    for _ in range(MANTISSA_STEPS):
        mid = lo + (hi - lo) * 0.5
        enough = count_ge(mid) >= cap
        lo = jnp.where(enough, mid, lo)
        hi = jnp.where(enough, hi, mid)
    above = aff >= hi
    tied = jnp.logical_and(aff >= lo, jnp.logical_not(above))
    need = cap - jnp.sum(above.astype(F32), axis=(1, 2), keepdims=True)

    ri = lax.broadcasted_iota(I32, (rows, rows), 0)
    ci = lax.broadcasted_iota(I32, (rows, rows), 1)
    nb_shift = nb.bit_length() - 1
    assert nb == 1 << nb_shift
    same_expert = lax.shift_right_logical(ri, nb_shift) == lax.shift_right_logical(ci, nb_shift)
    seg_lower = jnp.logical_and(same_expert, ci < ri).astype(BF16)
    tri_upper = (lax.broadcasted_iota(I32, (ROUTE_BLOCK, ROUTE_BLOCK), 0)
                 <= lax.broadcasted_iota(I32, (ROUTE_BLOCK, ROUTE_BLOCK), 1)).astype(BF16)

    tied_f = tied.astype(F32).reshape(rows, ROUTE_BLOCK)
    tied_rank = _segment_count(tied_f, seg_lower, tri_upper)[0] - tied_f
    chosen = jnp.logical_or(above, jnp.logical_and(tied, tied_rank.reshape(n_e, nb, ROUTE_BLOCK) < need))
    chosen_f = chosen.astype(F32).reshape(rows, ROUTE_BLOCK)
    incl, before, total = _segment_count(chosen_f, seg_lower, tri_upper)
    slot_ref[...] = jnp.where(chosen_f > 0, incl - 1.0, -1.0).astype(I32)
    before_ref[...] = before.astype(I32)
    total_ref[...] = total.astype(I32)


def _route_select(aff3, cap):
    n_e, nb, _ = aff3.shape
    rows = n_e * nb
    table = jax.ShapeDtypeStruct((rows, ROUTE_BLOCK), I32)
    return pl.pallas_call(
        functools.partial(_route_select_kernel, cap=cap),
        out_shape=[table, table, table],
        compiler_params=_params(None, 48),
        name="route_select",
    )(aff3)


def _route_index_kernel(slot_ref, aff_ref, before_ref, ends_ref, totals_ref, idx_ref, gate_ref):
    cap = idx_ref.shape[1]
    nb = slot_ref.shape[0]
    s_col = lax.broadcasted_iota(I32, (cap, nb), 0)
    done = ends_ref[0] <= s_col
    blk = jnp.sum(done.astype(F32), axis=1, keepdims=True)
    blk_first = jnp.sum(jnp.where(done, totals_ref[0].astype(F32), 0.0), axis=1, keepdims=True)
    onehot = (lax.broadcasted_iota(I32, (cap, nb), 1) == blk.astype(I32)).astype(BF16)
    slot = slot_ref[...]
    local = jnp.where(slot >= 0, slot - before_ref[...] + 1, 0).astype(BF16)
    picked = _dot(onehot, local)
    target = s_col[:, :1].astype(F32) - blk_first + 1.0
    match = picked == target
    lane = lax.broadcasted_iota(I32, match.shape, 1).astype(F32)
    tok = blk * ROUTE_BLOCK + jnp.sum(jnp.where(match, lane, 0.0), axis=1, keepdims=True)
    idx_ref[0] = tok.astype(I32)
    a = aff_ref[...]
    hi = a.astype(BF16)
    r1 = a - hi.astype(F32)
    mid = r1.astype(BF16)
    lo = (r1 - mid.astype(F32)).astype(BF16)
    a_rows = _dot(onehot, hi) + _dot(onehot, mid) + _dot(onehot, lo)
    gate_ref[0] = jnp.sum(jnp.where(match, a_rows, 0.0), axis=1, keepdims=True)


def _route_index(slot, aff2, before, ends, totals, cap):
    rows = slot.shape[0]
    n_e = N_EXPERTS
    nb = rows // n_e
    blocks = pl.BlockSpec((nb, ROUTE_BLOCK), lambda e: (e, 0))
    row = pl.BlockSpec((1, 1, nb), lambda e: (e, 0, 0))
    out = pl.BlockSpec((1, cap, 1), lambda e: (e, 0, 0))
    return pl.pallas_call(
        _route_index_kernel,
        grid=(n_e,),
        in_specs=[blocks, blocks, blocks, row, row],
        out_specs=[out, out],
        out_shape=[jax.ShapeDtypeStruct((n_e, cap, 1), I32), jax.ShapeDtypeStruct((n_e, cap, 1), F32)],
        compiler_params=_params(("parallel",), 48),
        name="route_index",
    )(slot, aff2, before, ends, totals)


def _ffn_kernel(idx_ref, xn_hbm, gate_ref, wg_ref, wu_ref, wd_ref, ye_ref, stage_ref, xe_ref, acc_ref, sem):
    e = pl.program_id(0)
    f = pl.program_id(1)
    n_e = pl.num_programs(0)
    cap = xe_ref.shape[0]
    rows_per_step = cap // (EXPERT_FF // wg_ref.shape[2])
    buf = e % 2

    def row_copy(expert, slot, dst_buf):
        tok = idx_ref[expert * cap + slot]
        return pltpu.make_async_copy(xn_hbm.at[pl.ds(tok, 1), :], stage_ref.at[dst_buf, pl.ds(slot, 1), :],
                                     sem.at[dst_buf])

    def wait_buffer(which):
        pltpu.make_async_copy(xn_hbm.at[pl.ds(0, cap), :], stage_ref.at[which], sem.at[which]).wait()

    @pl.when(f == 0)
    def _():
        @pl.when(e == 0)
        def _():
            def start(slot, c):
                row_copy(0, slot, 0).start()
                return c

            lax.fori_loop(0, cap, start, 0, unroll=8)

        wait_buffer(buf)
        xe_ref[...] = stage_ref[buf].astype(BF16)
        acc_ref[...] = jnp.zeros_like(acc_ref)

    xe = xe_ref[...]
    g = _dot(xe, wg_ref[0].astype(BF16))
    u = _dot(xe, wu_ref[0].astype(BF16))
    h = (_silu(g) * u).astype(BF16)
    acc_ref[...] += _dot(h, wd_ref[0].astype(BF16))

    nxt = jnp.minimum(e + 1, n_e - 1)
    for r in range(rows_per_step):
        row_copy(nxt, f * rows_per_step + r, 1 - buf).start()

    @pl.when(f == pl.num_programs(1) - 1)
    def _():
        ye_ref[0] = (acc_ref[...] * gate_ref[0]).astype(BF16)

        @pl.when(e == n_e - 1)
        def _():
            wait_buffer(1 - buf)


def _ffn(idx, xn, gates3, w_gate, w_up, w_down):
    e, cap = idx.shape
    tf = 256
    grid_spec = pltpu.PrefetchScalarGridSpec(
        num_scalar_prefetch=1,
        grid=(e, EXPERT_FF // tf),
        in_specs=[
            pl.BlockSpec(memory_space=pl.ANY),
            pl.BlockSpec((1, cap, 1), lambda ee, f, ix: (ee, 0, 0)),
            pl.BlockSpec((1, D_MODEL, tf), lambda ee, f, ix: (ee, 0, f)),
            pl.BlockSpec((1, D_MODEL, tf), lambda ee, f, ix: (ee, 0, f)),
            pl.BlockSpec((1, tf, D_MODEL), lambda ee, f, ix: (ee, f, 0)),
        ],
        out_specs=pl.BlockSpec((1, cap, D_MODEL), lambda ee, f, ix: (ee, 0, 0)),
        scratch_shapes=[
            pltpu.VMEM((2, cap, D_MODEL), F32),
            pltpu.VMEM((cap, D_MODEL), BF16),
            pltpu.VMEM((cap, D_MODEL), F32),
            pltpu.SemaphoreType.DMA((2,)),
        ],
    )
    return pl.pallas_call(
        _ffn_kernel,
        grid_spec=grid_spec,
        out_shape=jax.ShapeDtypeStruct((e, cap, D_MODEL), BF16),
        compiler_params=_params(("arbitrary", "arbitrary"), 56),
        name="moe_ffn",
    )(idx.reshape(-1), xn, gates3, w_gate, w_up, w_down)


def _combine_kernel(lo_ref, cnt_ref, slot_ref, x1_ref, lo_row_ref, cnt_row_ref, lo_col_ref, cnt_col_ref, ye_hbm,
                    y_ref, stage_ref, sem):
    i = pl.program_id(0)
    nt = pl.num_programs(0)
    n_e = ye_hbm.shape[0]
    tt = slot_ref.shape[0]
    shift = BF16_ROWS.bit_length() - 1

    def plan(tile):
        entries = []
        base = jnp.int32(0)
        for e in range(n_e):
            lo = lo_ref[tile * n_e + e]
            cnt = cnt_ref[tile * n_e + e]
            first = lax.shift_right_logical(lo, shift)
            groups = jnp.where(cnt > 0, lax.shift_right_logical(lo + cnt + (BF16_ROWS - 1), shift) - first, 0)
            entries.append((first * BF16_ROWS, groups, base))
            base = base + groups * BF16_ROWS
        return entries, base

    def for_each_group(tile, buf, fn):
        entries, _ = plan(tile)
        for e, (a0, groups, base) in enumerate(entries):
            def body(j, c, e=e, a0=a0, base=base):
                src = ye_hbm.at[e, pl.ds(pl.multiple_of(a0 + j * BF16_ROWS, BF16_ROWS), BF16_ROWS), :]
                dst = stage_ref.at[buf, pl.ds(pl.multiple_of(base + j * BF16_ROWS, BF16_ROWS), BF16_ROWS), :]
                fn(pltpu.make_async_copy(src, dst, sem.at[buf]))
                return c

            lax.fori_loop(0, groups, body, 0)

    buf = i % 2

    @pl.when(i == 0)
    def _():
        stage_ref[...] = jnp.zeros_like(stage_ref)
        for_each_group(0, 0, lambda cp: cp.start())

    @pl.when(i + 1 < nt)
    def _():
        for_each_group(i + 1, 1 - buf, lambda cp: cp.start())

    for_each_group(i, buf, lambda cp: cp.wait())

    def staged(lo, cnt):
        first = lax.shift_right_logical(lo, shift)
        groups = jnp.where(cnt > 0, lax.shift_right_logical(lo + cnt + (BF16_ROWS - 1), shift) - first, 0)
        return first * BF16_ROWS, groups * BF16_ROWS

    a0_row, rows_row = staged(lo_row_ref[0], cnt_row_ref[0])
    _, rows_col = staged(lo_col_ref[0], cnt_col_ref[0])
    ei = lax.broadcasted_iota(I32, (n_e, n_e), 0)
    ej = lax.broadcasted_iota(I32, (n_e, n_e), 1)
    base_row = _dot(rows_row.astype(F32).astype(BF16), (ei < ej).astype(BF16)).astype(I32)
    base_col = _dot((ej < ei).astype(BF16),
                    jnp.broadcast_to(rows_col, (n_e, LANE)).astype(F32).astype(BF16))[:, :1].astype(I32)
    slot = slot_ref[...]
    rel = jnp.where(slot >= 0, slot - a0_row + base_row, -1)
    rel_hi = lax.shift_right_arithmetic(rel, 6).astype(F32).astype(BF16)
    rel_lo = jnp.bitwise_and(rel, 63).astype(F32).astype(BF16)
    _, total = plan(i)
    y_ref[...] = x1_ref[...]

    def k_block(kb, c):
        first = kb * COMBINE_K
        col_e = lax.broadcasted_iota(I32, (n_e, COMBINE_K), 1) + first
        owner = jnp.logical_and(col_e >= base_col, col_e < base_col + rows_col).astype(BF16)
        want = 64.0 * _dot(rel_hi, owner) + _dot(rel_lo, owner)
        col_t = (lax.broadcasted_iota(I32, (tt, COMBINE_K), 1) + first).astype(F32)
        rows = pl.ds(pl.multiple_of(first, COMBINE_K), COMBINE_K)
        y_ref[...] += _dot((want == col_t).astype(BF16), stage_ref[buf, rows, :])
        return c

    lax.fori_loop(0, lax.shift_right_logical(total + (COMBINE_K - 1), COMBINE_K.bit_length() - 1), k_block, 0)


def _combine(lo_flat, cnt_flat, slot, x1, ye):
    t = x1.shape[0]
    n_e = ye.shape[0]
    tt = ROUTE_BLOCK
    stage_rows = n_e * (tt + BF16_ROWS)
    assert stage_rows % COMBINE_K == 0
    nt = t // tt
    lo2 = lo_flat.reshape(nt, n_e)
    cnt2 = cnt_flat.reshape(nt, n_e)
    row_spec = pl.BlockSpec((1, 1, n_e), lambda i, lo, cnt: (i, 0, 0))
    col_spec = pl.BlockSpec((1, n_e, 1), lambda i, lo, cnt: (i, 0, 0))
    grid_spec = pltpu.PrefetchScalarGridSpec(
        num_scalar_prefetch=2,
        grid=(nt,),
        in_specs=[
            pl.BlockSpec((tt, n_e), lambda i, lo, cnt: (i, 0)),
            pl.BlockSpec((tt, D_MODEL), lambda i, lo, cnt: (i, 0)),
            row_spec, row_spec, col_spec, col_spec,
            pl.BlockSpec(memory_space=pl.ANY),
        ],
        out_specs=pl.BlockSpec((tt, D_MODEL), lambda i, lo, cnt: (i, 0)),
        scratch_shapes=[
            pltpu.VMEM((2, stage_rows, D_MODEL), BF16),
            pltpu.SemaphoreType.DMA((2,)),
        ],
    )
    return pl.pallas_call(
        _combine_kernel,
        grid_spec=grid_spec,
        out_shape=jax.ShapeDtypeStruct((t, D_MODEL), F32),
        compiler_params=_params(("arbitrary",), 48),
        name="moe_combine",
    )(lo_flat, cnt_flat, slot, x1, lo2[:, None, :], cnt2[:, None, :], lo2[:, :, None], cnt2[:, :, None], ye)


def _rope_tables(s):
    half = QK_ROPE // 2
    inv_freq = ROPE_THETA ** (-jnp.arange(half, dtype=F32) / half)
    ang = jnp.arange(s, dtype=F32)[:, None] * inv_freq[None, :]
    cos = jnp.cos(ang)
    sin = jnp.sin(ang)
    zeros = jnp.zeros((s, LANE - QK_ROPE), F32)
    return jnp.concatenate([cos, cos, zeros], axis=1), jnp.concatenate([-sin, sin, zeros], axis=1)


def _prepare_weights(norm1_g, w_in, q_a_norm_g, w_q_b, kv_a_norm_g, w_kv_b, q_norm_g, k_norm_g, conv_w,
                     a_log_fwd, a_log_bwd, dt_bias_fwd, dt_bias_bwd, gdn_norm_g, w_o, norm2_g, w_router):
    o_kv = Q_LORA
    o_kr = o_kv + KV_LORA
    o_qkv = o_kr + QK_ROPE
    o_z = o_qkv + QKV_GDN
    o_g = o_z + GDN_WIDTH
    wmisc = jnp.concatenate(
        [w_in[:, o_kr:o_qkv], w_in[:, o_g:], jnp.zeros((D_MODEL, LANE - QK_ROPE - 4 * GDN_HEADS), F32)],
        axis=1).astype(BF16)
    wq = jnp.pad(w_q_b.reshape(Q_LORA, MLA_HEADS, QK_HEAD), ((0, 0), (0, 0), (0, QK_PAD - QK_HEAD)))
    wq = wq.reshape(Q_LORA, MLA_HEADS * QK_PAD).astype(BF16)
    pad_g = lambda g: jnp.pad(g, (0, QK_PAD - QK_HEAD)).reshape(1, QK_PAD)
    alog = jnp.concatenate([a_log_fwd, a_log_bwd]).reshape(2 * GDN_HEADS, 1, 1)
    dtb = jnp.concatenate([dt_bias_fwd, dt_bias_bwd]).reshape(2 * GDN_HEADS, 1, 1)
    return dict(
        g1=norm1_g.reshape(1, D_MODEL), wz=w_in[:, o_z:o_g].astype(BF16), wlat=w_in[:, :o_kr].astype(BF16),
        wqkv=w_in[:, o_qkv:o_z].astype(BF16), wmisc=wmisc,
        gqa=q_a_norm_g.reshape(1, Q_LORA), gkva=kv_a_norm_g.reshape(1, KV_LORA),
        wq=wq, wkv=w_kv_b.astype(BF16), gq=pad_g(q_norm_g), gk=pad_g(k_norm_g),
        conv_w=conv_w,
        alog=jnp.broadcast_to(alog, (2 * GDN_HEADS, 1, LANE)), dtb=jnp.broadcast_to(dtb, (2 * GDN_HEADS, 1, LANE)),
        gn=gdn_norm_g.reshape(1, GDN_DV), wo=w_o.astype(BF16), g2=norm2_g.reshape(1, D_MODEL), wr=w_router,
    )


def _route(aff, cap):
    t, n_e = aff.shape
    nb = t // ROUTE_BLOCK
    aff3 = aff.T.reshape(n_e, nb, ROUTE_BLOCK)
    slot, before, total = _route_select(aff3, cap)
    before_b = before[:, 0].reshape(n_e, nb)
    total_b = total[:, 0].reshape(n_e, nb)
    idx, gates = _route_index(slot, aff3.reshape(n_e * nb, ROUTE_BLOCK), before,
                              (before_b + total_b).reshape(n_e, 1, nb), total_b.reshape(n_e, 1, nb), cap)
    slot_by_token = slot.reshape(n_e, t).T
    return idx.reshape(n_e, cap), gates, slot_by_token, before_b.T.reshape(-1), total_b.T.reshape(-1)


def _encoder_layer(x, p, w_gate, w_up, w_down):
    bt, s, _ = x.shape
    t = bt * s
    x2d = x.reshape(t, D_MODEL)
    oa, ob = _in_proj(x2d, p["g1"], p["wz"], p["wlat"], p["wqkv"], p["wmisc"])

    cos, sin = _rope_tables(s)
    q, k, v = _mla_prep(oa, ob, cos, sin, p["gqa"], p["gkva"], p["wq"], p["wkv"], p["gq"], p["gk"], bt, s)
    mla_out = _attention(q, k, v)

    ob3 = ob.reshape(bt, s, NB)
    qkv_h = _conv(ob3, p["conv_w"])
    g_raw = ob3[:, :, QKV_GDN + QK_ROPE:QKV_GDN + QK_ROPE + 4 * GDN_HEADS]
    g_t = jnp.swapaxes(g_raw, 1, 2).reshape(bt, 4 * GDN_HEADS, s // LANE, LANE)
    g_out = _gates(g_t, p["alog"], p["dtb"]).reshape(bt, 4, GDN_HEADS, s)
    g_rows = g_out[:, :2].reshape(bt, 2 * GDN_HEADS, s // CHUNK, CHUNK)
    g_cols = jnp.transpose(g_out, (0, 2, 3, 1))
    o_fwd, o_bwd = _gdn_scan(_gdn_prep(qkv_h, g_rows, g_cols), g_rows)

    x1, xn, aff = _out_proj(x2d, mla_out.reshape(t, MLA_WIDTH), o_fwd.reshape(t, GDN_WIDTH),
                            o_bwd.reshape(t, GDN_WIDTH), oa, p["gn"], p["wo"], p["g2"], p["wr"])

    cap = max(1, CAPACITY_FACTOR * t // N_EXPERTS)
    idx, gates, slot, lo, cnt = _route(aff, cap)
    ye = _ffn(idx, xn, gates, w_gate, w_up, w_down)
    y = _combine(lo, cnt, slot, x1, ye)
    return y.reshape(bt, s, D_MODEL)


def kernel(x_prompt, x_sample, norm1_g, w_in, q_a_norm_g, w_q_b, kv_a_norm_g, w_kv_b, q_norm_g, k_norm_g, conv_w,
           a_log_fwd, a_log_bwd, dt_bias_fwd, dt_bias_bwd, gdn_norm_g, w_o, norm2_g, w_router, w_gate, w_up, w_down):
    y_prompt = x_prompt
    y_sample = x_sample
    for l in range(norm1_g.shape[0]):
        p = _prepare_weights(norm1_g[l], w_in[l], q_a_norm_g[l], w_q_b[l], kv_a_norm_g[l], w_kv_b[l], q_norm_g[l],
                             k_norm_g[l], conv_w[l], a_log_fwd[l], a_log_bwd[l], dt_bias_fwd[l], dt_bias_bwd[l],
                             gdn_norm_g[l], w_o[l], norm2_g[l], w_router[l])
        y_prompt = _encoder_layer(y_prompt, p, w_gate[l], w_up[l], w_down[l])
        y_sample = _encoder_layer(y_sample, p, w_gate[l], w_up[l], w_down[l])
    return (y_prompt, y_sample)
```

```python
import functools
import math

import jax
import jax.numpy as jnp
from jax import lax
from jax.experimental import pallas as pl
from jax.experimental.pallas import tpu as pltpu

F32 = jnp.float32
BF16 = jnp.bfloat16
I32 = jnp.int32

D_MODEL = 2048
MLA_HEADS = 8
Q_LORA = 512
KV_LORA = 256
QK_NOPE = 128
QK_ROPE = 64
QK_HEAD = QK_NOPE + QK_ROPE
QK_PAD = 256
V_HEAD = 128
V_PAD = 256
ROPE_THETA = 10000.0
GDN_HEADS = 8
GDN_DK = 128
GDN_DV = 128
QKV_GDN = 2 * GDN_HEADS * GDN_DK + GDN_HEADS * GDN_DV
GDN_WIDTH = GDN_HEADS * GDN_DV
MLA_WIDTH = MLA_HEADS * V_HEAD
CONV_W = 5
CHUNK = 64
N_EXPERTS = 16
EXPERT_FF = 1024
CAPACITY_FACTOR = 2
EPS = 1e-6

LANE = 128
SUBLANES = 8
BF16_ROWS = 16
MIB = 1024 * 1024

NA = GDN_WIDTH + Q_LORA + KV_LORA
NB = QKV_GDN + LANE
QLAT_COL_BLOCK = GDN_WIDTH // Q_LORA
KVLAT_COL_BLOCK = (GDN_WIDTH + Q_LORA) // KV_LORA
MISC_COL_BLOCK = QKV_GDN // LANE
PREP_CHUNKS = 16
SCAN_CHUNKS = 8
ATTN_SUB = 256
ROUTE_BLOCK = LANE
COMBINE_K = 256
EXP_SPAN = 126
MANTISSA_STEPS = 40

NN = (((1,), (0,)), ((), ()))
NT = (((1,), (1,)), ((), ()))
TN = (((0,), (0,)), ((), ()))


def _params(semantics, vmem_mib):
    return pltpu.CompilerParams(dimension_semantics=semantics, vmem_limit_bytes=vmem_mib * MIB)


def _resident(shape):
    nd = len(shape)
    return pl.BlockSpec(shape, lambda *_: (0,) * nd, pipeline_mode=pl.Buffered(1))


def _dot(a, b, dims=NN):
    return lax.dot_general(a, b, dims, preferred_element_type=F32)


def _dotb(a, b, dims=NN):
    return lax.dot_general(a.astype(BF16), b.astype(BF16), dims, preferred_element_type=F32)


def _split(x):
    hi = x.astype(BF16)
    lo = (x - hi.astype(F32)).astype(BF16)
    return hi, lo


def _dot3(a, b, dims=NN):
    ah, al = _split(a)
    bh, bl = _split(b)
    return _dot(ah, bh, dims) + (_dot(ah, bl, dims) + _dot(al, bh, dims))


def _silu(x):
    return x * jax.nn.sigmoid(x)


def _in_proj_kernel(x_ref, g_ref, wz_ref, wlat_ref, wqkv_ref, wmisc_ref, oa_ref, ob_ref):
    x = x_ref[...]
    ms = jnp.mean(x * x, axis=-1, keepdims=True)
    h = (x * lax.rsqrt(ms + EPS) * g_ref[...]).astype(BF16)
    oa_ref[:, :GDN_WIDTH] = _dot(h, wz_ref[...]).astype(BF16)
    oa_ref[:, GDN_WIDTH:] = _dot(h, wlat_ref[...]).astype(BF16)
    ob_ref[:, :QKV_GDN] = _dot(h, wqkv_ref[...])
    ob_ref[:, QKV_GDN:] = _dot(h, wmisc_ref[...])


def _in_proj(x2d, g, wz, wlat, wqkv, wmisc):
    t = x2d.shape[0]
    tm = min(512, t)
    return pl.pallas_call(
        _in_proj_kernel,
        grid=(t // tm,),
        in_specs=[
            pl.BlockSpec((tm, D_MODEL), lambda i: (i, 0)),
            _resident((1, D_MODEL)),
            _resident(wz.shape),
            _resident(wlat.shape),
            _resident(wqkv.shape),
            _resident(wmisc.shape),
        ],
        out_specs=[
            pl.BlockSpec((tm, NA), lambda i: (i, 0)),
            pl.BlockSpec((tm, NB), lambda i: (i, 0)),
        ],
        out_shape=[
            jax.ShapeDtypeStruct((t, NA), BF16),
            jax.ShapeDtypeStruct((t, NB), F32),
        ],
        compiler_params=_params(("parallel",), 56),
        name="in_proj",
    )(x2d, g, wz, wlat, wqkv, wmisc)


def _mla_prep_kernel(ql_ref, kvl_ref, misc_ref, cos_ref, sin_ref, gqa_ref, gkva_ref, wq_ref, wkv_ref,
                     gq_ref, gk_ref, q_ref, k_ref, v_ref):
    def norm(x, g):
        return x * lax.rsqrt(jnp.mean(x * x, axis=-1, keepdims=True) + EPS) * g

    qn = norm(ql_ref[...].astype(F32), gqa_ref[...]).astype(BF16)
    kvn = norm(kvl_ref[...].astype(F32), gkva_ref[...]).astype(BF16)
    q = _dot(qn, wq_ref[...])
    kv = _dot(kvn, wkv_ref[...])

    cos = cos_ref[...]
    sin = sin_ref[...]
    lane = lax.broadcasted_iota(I32, cos.shape, 1)

    def rope(t):
        rot = jnp.where(lane < QK_ROPE // 2, pltpu.roll(t, LANE - QK_ROPE // 2, 1), pltpu.roll(t, QK_ROPE // 2, 1))
        return t * cos + rot * sin

    gq = gq_ref[...]
    gk = gk_ref[...]
    kr = jnp.where(lane < QK_ROPE, misc_ref[...], 0.0)
    kr_ssq = jnp.sum(kr * kr, axis=-1, keepdims=True)
    kr_rot = rope(kr * gk[:, QK_NOPE:])
    q_scale = QK_HEAD ** -0.5
    ones_col = jnp.where(lane == 0, 1.0, 0.0).astype(BF16)
    for h in range(MLA_HEADS):
        qh = q[:, h * QK_PAD:(h + 1) * QK_PAD]
        sc = lax.rsqrt(jnp.sum(qh * qh, axis=-1, keepdims=True) * (1.0 / QK_HEAD) + EPS) * q_scale
        q_ref[0, h, :, :QK_NOPE] = (qh[:, :QK_NOPE] * gq[:, :QK_NOPE] * sc).astype(BF16)
        q_ref[0, h, :, QK_NOPE:] = (rope(qh[:, QK_NOPE:] * gq[:, QK_NOPE:]) * sc).astype(BF16)
        kn = kv[:, h * 256:h * 256 + QK_NOPE]
        sk = lax.rsqrt((jnp.sum(kn * kn, axis=-1, keepdims=True) + kr_ssq) * (1.0 / QK_HEAD) + EPS)
        k_ref[0, h, :, :QK_NOPE] = (kn * gk[:, :QK_NOPE] * sk).astype(BF16)
        k_ref[0, h, :, QK_NOPE:] = (kr_rot * sk).astype(BF16)
        v_ref[0, h, :, :V_HEAD] = kv[:, h * 256 + QK_NOPE:(h + 1) * 256].astype(BF16)
        v_ref[0, h, :, V_HEAD:] = ones_col


def _mla_prep(oa, ob, cos, sin, gqa, gkva, wq, wkv, gq, gk, bt, s):
    tm = min(512, s)
    nt = s // tm
    return pl.pallas_call(
        _mla_prep_kernel,
        grid=(bt, nt),
        in_specs=[
            pl.BlockSpec((tm, Q_LORA), lambda b, i: (b * nt + i, QLAT_COL_BLOCK)),
            pl.BlockSpec((tm, KV_LORA), lambda b, i: (b * nt + i, KVLAT_COL_BLOCK)),
            pl.BlockSpec((tm, LANE), lambda b, i: (b * nt + i, MISC_COL_BLOCK)),
            pl.BlockSpec((tm, LANE), lambda b, i: (i, 0)),
            pl.BlockSpec((tm, LANE), lambda b, i: (i, 0)),
            _resident(gqa.shape),
            _resident(gkva.shape),
            _resident(wq.shape),
            _resident(wkv.shape),
            _resident(gq.shape),
            _resident(gk.shape),
        ],
        out_specs=[
            pl.BlockSpec((1, MLA_HEADS, tm, QK_PAD), lambda b, i: (b, 0, i, 0)),
            pl.BlockSpec((1, MLA_HEADS, tm, QK_PAD), lambda b, i: (b, 0, i, 0)),
            pl.BlockSpec((1, MLA_HEADS, tm, V_PAD), lambda b, i: (b, 0, i, 0)),
        ],
        out_shape=[
            jax.ShapeDtypeStruct((bt, MLA_HEADS, s, QK_PAD), BF16),
            jax.ShapeDtypeStruct((bt, MLA_HEADS, s, QK_PAD), BF16),
            jax.ShapeDtypeStruct((bt, MLA_HEADS, s, V_PAD), BF16),
        ],
        compiler_params=_params(("parallel", "parallel"), 48),
        name="mla_prep",
    )(oa, oa, ob, cos, sin, gqa, gkva, wq, wkv, gq, gk)


def _attn_kernel(q_ref, k_ref, v_ref, o_ref):
    k = k_ref[0, 0]
    v = v_ref[0, 0]
    parts = [slice(r, r + ATTN_SUB) for r in range(0, q_ref.shape[2], ATTN_SUB)]
    scores = [_dot(q_ref[0, 0, rows, :], k, NT) for rows in parts]
    for rows, s in zip(parts, scores):
        m = jnp.max(s, axis=-1, keepdims=True)
        p = jnp.exp((s - m).astype(BF16))
        o = _dot(p, v)
        o_ref[0, rows, :] = (o[:, :V_HEAD] / o[:, V_HEAD:V_HEAD + 1]).astype(BF16)


def _attention(q, k, v):
    bt, h, s, _ = q.shape
    tq = min(4 * ATTN_SUB, s)
    return pl.pallas_call(
        _attn_kernel,
        grid=(bt, h, s // tq),
        in_specs=[
            pl.BlockSpec((1, 1, tq, QK_PAD), lambda b, hh, i: (b, hh, i, 0)),
            pl.BlockSpec((1, 1, s, QK_PAD), lambda b, hh, i: (b, hh, 0, 0)),
            pl.BlockSpec((1, 1, s, V_PAD), lambda b, hh, i: (b, hh, 0, 0)),
        ],
        out_specs=pl.BlockSpec((1, tq, V_HEAD), lambda b, hh, i: (b, i, hh)),
        out_shape=jax.ShapeDtypeStruct((bt, s, MLA_WIDTH), BF16),
        compiler_params=_params(("parallel", "parallel", "parallel"), 48),
        name="attention",
    )(q, k, v)


def _conv_kernel(x_ref, w_ref, o_ref, pad_ref):
    c = pl.program_id(1)
    x = x_ref[0]
    s = x.shape[0]
    w = w_ref[...]
    pad = CONV_W // 2
    halo = jnp.zeros((SUBLANES, LANE), F32)
    pad_ref[:SUBLANES, :] = halo
    pad_ref[SUBLANES + s:, :] = halo
    pad_ref[SUBLANES:SUBLANES + s, :] = x
    acc = x * w[pad:pad + 1]
    for d in range(-pad, pad + 1):
        if d != 0:
            acc = acc + pad_ref[SUBLANES + d:SUBLANES + d + s, :] * w[pad + d:pad + d + 1]
    y = _silu(acc)
    inv = lax.rsqrt(jnp.sum(y * y, axis=-1, keepdims=True) + EPS)
    is_q = c < GDN_HEADS
    is_qk = c < 2 * GDN_HEADS
    o_ref[0, 0] = y * (jnp.where(is_qk, inv, 1.0) * jnp.where(is_q, GDN_DK ** -0.5, 1.0))


def _conv(ob3, conv_w):
    bt, s, _ = ob3.shape
    nblk = QKV_GDN // LANE
    return pl.pallas_call(
        _conv_kernel,
        grid=(bt, nblk),
        in_specs=[
            pl.BlockSpec((1, s, LANE), lambda b, c: (b, 0, c)),
            pl.BlockSpec((CONV_W, LANE), lambda b, c: (0, c)),
        ],
        out_specs=pl.BlockSpec((1, 1, s, LANE), lambda b, c: (b, c, 0, 0)),
        out_shape=jax.ShapeDtypeStruct((bt, nblk, s, LANE), F32),
        scratch_shapes=[pltpu.VMEM((s + 2 * SUBLANES, LANE), F32)],
        compiler_params=_params(("parallel", "parallel"), 48),
        name="gdn_conv",
    )(ob3, conv_w)


def _gates_kernel(x_ref, alog_ref, dtb_ref, o_ref):
    x = x_ref[0]
    nh = GDN_HEADS
    a = x[:2 * nh] + dtb_ref[...]
    softplus = jnp.maximum(a, 0.0) + jnp.log1p(jnp.exp(-jnp.abs(a)))
    g = -jnp.exp(alog_ref[...]) * softplus
    beta = jax.nn.sigmoid(x[2 * nh:])
    gf = g[:nh]
    gb = g[nh:]
    pos = lax.broadcasted_iota(I32, gf.shape, 2) % CHUNK
    sh = 1
    while sh < CHUNK:
        gf = gf + jnp.where(pos >= sh, pltpu.roll(gf, sh, 2), 0.0)
        gb = gb + jnp.where(pos < CHUNK - sh, pltpu.roll(gb, LANE - sh, 2), 0.0)
        sh *= 2
    o_ref[0, :nh] = gf
    o_ref[0, nh:2 * nh] = gb
    o_ref[0, 2 * nh:] = beta


def _gates(gt, alog, dtb):
    bt, r, sl, _ = gt.shape
    return pl.pallas_call(
        _gates_kernel,
        grid=(bt,),
        in_specs=[
            pl.BlockSpec((1, r, sl, LANE), lambda b: (b, 0, 0, 0)),
            _resident(alog.shape),
            _resident(dtb.shape),
        ],
        out_specs=pl.BlockSpec((1, r, sl, LANE), lambda b: (b, 0, 0, 0)),
        out_shape=jax.ShapeDtypeStruct(gt.shape, F32),
        compiler_params=_params(("parallel",), 32),
        name="gdn_gates",
    )(gt, alog, dtb)


def _unit_tri_inverses(lmats, eye):
    ps = [eye - l for l in lmats]
    ms = [_dotb(l, l) for l in lmats]
    n = 2
    while n < CHUNK // 2:
        ps = [p + _dotb(p, m) for p, m in zip(ps, ms)]
        ms = [_dotb(m, m) for m in ms]
        n *= 2
    return [p + _dotb(p, m) for p, m in zip(ps, ms)]


def _gdn_prep_kernel(q_ref, k_ref, v_ref, grf_ref, grb_ref, gc_ref,
                     wqf_ref, uf_ref, kdf_ref, inf_ref, wqb_ref, ub_ref, kdb_ref, inb_ref):
    ri = lax.broadcasted_iota(I32, (CHUNK, CHUNK), 0)
    ci = lax.broadcasted_iota(I32, (CHUNK, CHUNK), 1)
    eye = (ri == ci).astype(F32)
    dirs = (
        (0, grf_ref, ri >= ci, ri > ci, CHUNK - 1, (wqf_ref, uf_ref, kdf_ref, inf_ref)),
        (1, grb_ref, ri <= ci, ri < ci, 0, (wqb_ref, ub_ref, kdb_ref, inb_ref)),
    )
    lmats = []
    work = []
    for c in range(PREP_CHUNKS):
        rows = slice(c * CHUNK, (c + 1) * CHUNK)
        q = q_ref[0, 0, rows, :]
        k = k_ref[0, 0, rows, :]
        v = v_ref[0, 0, rows, :]
        g4 = gc_ref[0, 0, rows, :]
        kk = _dotb(k, k, NT)
        qk = _dotb(q, k, NT)
        for col, grow_ref, incl, strict, last_row, outs in dirs:
            gcol = g4[:, col:col + 1]
            beta = g4[:, 2 + col:3 + col]
            decay = jnp.exp(jnp.where(incl, gcol - grow_ref[0, 0, c:c + 1, :], -jnp.inf))
            lmats.append(jnp.where(strict, beta * kk * decay, 0.0))
            e_gc = jnp.exp(gcol)
            rhs = jnp.concatenate([v * beta, k * (beta * e_gc)], axis=-1)
            k_dec = k * jnp.exp(g4[last_row:last_row + 1, col:col + 1] - gcol)
            intra = jnp.where(incl, qk * decay, 0.0)
            work.append((c, rows, rhs, q * e_gc, k_dec, intra, outs))
    tinvs = _unit_tri_inverses(lmats, eye)
    for tinv, (c, rows, rhs, q_dec, k_dec, intra, (wq_ref, u_ref, kd_ref, in_ref)) in zip(tinvs, work):
        sol = _dotb(tinv, rhs)
        u_ref[0, 0, rows, :] = sol[:, :GDN_DV].astype(BF16)
        wq_ref[0, 0, c, :CHUNK, :] = sol[:, GDN_DV:].astype(BF16)
        wq_ref[0, 0, c, CHUNK:, :] = q_dec.astype(BF16)
        kd_ref[0, 0, rows, :] = k_dec.astype(BF16)
        in_ref[0, 0, rows, :] = intra.astype(BF16)


def _gdn_prep(qkv_h, g_rows, g_cols):
    bt, _, s, _ = qkv_h.shape
    nh = GDN_HEADS
    rb = PREP_CHUNKS * CHUNK
    n_chunks = s // CHUNK
    head = lambda off: (lambda b, h, i: (b, off + h, i, 0))
    seq_spec = lambda w: pl.BlockSpec((1, 1, rb, w), head(0))
    wq_spec = pl.BlockSpec((1, 1, PREP_CHUNKS, 2 * CHUNK, GDN_DK), lambda b, h, i: (b, h, i, 0, 0))
    out_specs = [wq_spec, seq_spec(GDN_DV), seq_spec(GDN_DK), seq_spec(CHUNK)]
    out_shape = [
        jax.ShapeDtypeStruct((bt, nh, n_chunks, 2 * CHUNK, GDN_DK), BF16),
        jax.ShapeDtypeStruct((bt, nh, s, GDN_DV), BF16),
        jax.ShapeDtypeStruct((bt, nh, s, GDN_DK), BF16),
        jax.ShapeDtypeStruct((bt, nh, s, CHUNK), BF16),
    ]
    return pl.pallas_call(
        _gdn_prep_kernel,
        grid=(bt, nh, s // rb),
        in_specs=[
            pl.BlockSpec((1, 1, rb, GDN_DK), head(0)),
            pl.BlockSpec((1, 1, rb, GDN_DK), head(nh)),
            pl.BlockSpec((1, 1, rb, GDN_DV), head(2 * nh)),
            pl.BlockSpec((1, 1, PREP_CHUNKS, CHUNK), head(0)),
            pl.BlockSpec((1, 1, PREP_CHUNKS, CHUNK), head(nh)),
            pl.BlockSpec((1, 1, rb, 4), head(0)),
        ],
        out_specs=out_specs + out_specs,
        out_shape=out_shape + out_shape,
        compiler_params=_params(("parallel", "parallel", "parallel"), 48),
        name="gdn_prep",
    )(qkv_h, qkv_h, qkv_h, g_rows, g_rows, g_cols)


def _gdn_scan_kernel(wqf_ref, uf_ref, kdf_ref, inf_ref, grf_ref, wqb_ref, ub_ref, kdb_ref, inb_ref, grb_ref,
                     of_ref, ob_ref, state_ref):
    @pl.when(pl.program_id(1) == 0)
    def _():
        state_ref[...] = jnp.zeros_like(state_ref)

    def step(c, carry):
        dirs = (
            (0, c, wqf_ref, uf_ref, kdf_ref, inf_ref, grf_ref, CHUNK - 1, of_ref),
            (1, SCAN_CHUNKS - 1 - c, wqb_ref, ub_ref, kdb_ref, inb_ref, grb_ref, 0, ob_ref),
        )
        chains = [(h,) + d for h in range(GDN_HEADS) for d in dirs]
        states = [state_ref[d, h] for h, d, *_ in chains]
        rs = [_dot(wq_ref[0, h, cc], s.astype(BF16)) for (h, _, cc, wq_ref, *_), s in zip(chains, states)]
        for (h, d, cc, _, u_ref, kd_ref, in_ref, g_ref, last_lane, o_ref), s, r in zip(chains, states, rs):
            rows = pl.ds(pl.multiple_of(cc * CHUNK, CHUNK), CHUNK)
            v_new = (u_ref[0, h, rows, :].astype(F32) - r[:CHUNK]).astype(BF16)
            o = r[CHUNK:] + _dot(in_ref[0, h, rows, :], v_new)
            o_ref[0, rows, h * GDN_DV:(h + 1) * GDN_DV] = o.astype(BF16)
            g_last = g_ref[0, d * GDN_HEADS + h, pl.ds(cc, 1), :][:, last_lane:last_lane + 1]
            state_ref[d, h] = s * jnp.exp(g_last) + _dot(kd_ref[0, h, rows, :], v_new, TN)
        return carry

    lax.fori_loop(0, SCAN_CHUNKS, step, 0)


def _gdn_scan(prep, g_rows):
    wqf, uf, kdf, inf_, wqb, ub, kdb, inb = prep
    bt, nh, s, _ = uf.shape
    rb = SCAN_CHUNKS * CHUNK
    nblk = s // rb
    fwd = lambda b, i: (b, 0, i, 0)
    bwd = lambda b, i: (b, 0, nblk - 1 - i, 0)

    def specs(idx):
        idx5 = lambda b, i: idx(b, i) + (0,)
        return [
            pl.BlockSpec((1, nh, SCAN_CHUNKS, 2 * CHUNK, GDN_DK), idx5),
            pl.BlockSpec((1, nh, rb, GDN_DV), idx),
            pl.BlockSpec((1, nh, rb, GDN_DK), idx),
            pl.BlockSpec((1, nh, rb, CHUNK), idx),
            pl.BlockSpec((1, 2 * nh, SCAN_CHUNKS, CHUNK), idx),
        ]

    out = jax.ShapeDtypeStruct((bt, s, GDN_WIDTH), BF16)
    return pl.pallas_call(
        _gdn_scan_kernel,
        grid=(bt, nblk),
        in_specs=specs(fwd) + specs(bwd),
        out_specs=[
            pl.BlockSpec((1, rb, GDN_WIDTH), lambda b, i: (b, i, 0)),
            pl.BlockSpec((1, rb, GDN_WIDTH), lambda b, i: (b, nblk - 1 - i, 0)),
        ],
        out_shape=[out, out],
        scratch_shapes=[pltpu.VMEM((2, nh, GDN_DK, GDN_DV), F32)],
        compiler_params=_params(("parallel", "arbitrary"), 48),
        name="gdn_scan",
    )(wqf, uf, kdf, inf_, g_rows, wqb, ub, kdb, inb, g_rows)


def _out_proj_kernel(x_ref, mla_ref, of_ref, ob_ref, z_ref, gn_ref, wo_ref, g_ref, wr_ref, x1_ref, xn_ref, aff_ref,
                     gdn_ref):
    for h in range(GDN_HEADS):
        cols = slice(h * GDN_DV, (h + 1) * GDN_DV)
        o = of_ref[:, cols].astype(F32) + ob_ref[:, cols].astype(F32)
        on = o * lax.rsqrt(jnp.mean(o * o, axis=-1, keepdims=True) + EPS) * gn_ref[...]
        gdn_ref[:, cols] = (on * _silu(z_ref[:, cols].astype(F32))).astype(BF16)
    y = x_ref[...] + _dot(mla_ref[...], wo_ref[:MLA_WIDTH, :]) + _dot(gdn_ref[...], wo_ref[MLA_WIDTH:, :])
    x1_ref[...] = y
    xn = y * lax.rsqrt(jnp.mean(y * y, axis=-1, keepdims=True) + EPS) * g_ref[...]
    xn_ref[...] = xn
    logits = _dot3(xn, wr_ref[...])
    e = jnp.exp(logits - jnp.max(logits, axis=-1, keepdims=True))
    aff_ref[...] = e / jnp.sum(e, axis=-1, keepdims=True)


def _out_proj(x2d, mla, o_fwd, o_bwd, oa, gn, wo, g2, wr):
    t = x2d.shape[0]
    tm = min(256, t)
    rows = lambda w: pl.BlockSpec((tm, w), lambda i: (i, 0))
    return pl.pallas_call(
        _out_proj_kernel,
        grid=(t // tm,),
        in_specs=[rows(D_MODEL), rows(MLA_WIDTH), rows(GDN_WIDTH), rows(GDN_WIDTH), rows(GDN_WIDTH),
                  _resident(gn.shape), _resident(wo.shape), _resident(g2.shape), _resident(wr.shape)],
        out_specs=[rows(D_MODEL), rows(D_MODEL), rows(N_EXPERTS)],
        out_shape=[
            jax.ShapeDtypeStruct((t, D_MODEL), F32),
            jax.ShapeDtypeStruct((t, D_MODEL), F32),
            jax.ShapeDtypeStruct((t, N_EXPERTS), F32),
        ],
        scratch_shapes=[pltpu.VMEM((tm, GDN_WIDTH), BF16)],
        compiler_params=_params(("parallel",), 48),
        name="out_proj",
    )(x2d, mla, o_fwd, o_bwd, oa, gn, wo, g2, wr)


def _segment_count(flags, seg_lower, tri_upper):
    within = _dot(flags.astype(BF16), tri_upper)
    total = jnp.broadcast_to(within[:, ROUTE_BLOCK - 1:], within.shape)
    before = _dot(seg_lower, total.astype(BF16))
    return within + before, before, total


def _route_select_kernel(aff_ref, slot_ref, before_ref, total_ref, *, cap):
    n_e, nb, _ = aff_ref.shape
    rows = n_e * nb
    aff = aff_ref[...]

    def count_ge(t):
        return jnp.sum((aff >= t).astype(F32), axis=(1, 2), keepdims=True)

    e_hi = jnp.full((n_e, 1, 1), -1.0, F32)
    e_lo = jnp.full((n_e, 1, 1), float(EXP_SPAN), F32)
    for _ in range(EXP_SPAN.bit_length()):
        e_mid = jnp.floor((e_hi + e_lo) * 0.5)
        enough = count_ge(jnp.exp2(-e_mid)) >= cap
        e_lo = jnp.where(enough, e_mid, e_lo)
        e_hi = jnp.where(enough, e_hi, e_mid)
    lo = jnp.where(e_lo >= EXP_SPAN, 0.0, jnp.exp2(-e_lo))
    hi = jnp.exp2(-e_hi)
    for _ in range(MANTISSA_STEPS):
        mid = lo + (hi - lo) * 0.5
        enough = count_ge(mid) >= cap
        lo = jnp.where(enough, mid, lo)
        hi = jnp.where(enough, hi, mid)
    above = aff >= hi
    tied = jnp.logical_and(aff >= lo, jnp.logical_not(above))
    need = cap - jnp.sum(above.astype(F32), axis=(1, 2), keepdims=True)

    ri = lax.broadcasted_iota(I32, (rows, rows), 0)
    ci = lax.broadcasted_iota(I32, (rows, rows), 1)
    nb_shift = nb.bit_length() - 1
    assert nb == 1 << nb_shift
    same_expert = lax.shift_right_logical(ri, nb_shift) == lax.shift_right_logical(ci, nb_shift)
    seg_lower = jnp.logical_and(same_expert, ci < ri).astype(BF16)
    tri_upper = (lax.broadcasted_iota(I32, (ROUTE_BLOCK, ROUTE_BLOCK), 0)
                 <= lax.broadcasted_iota(I32, (ROUTE_BLOCK, ROUTE_BLOCK), 1)).astype(BF16)

    tied_f = tied.astype(F32).reshape(rows, ROUTE_BLOCK)
    tied_rank = _segment_count(tied_f, seg_lower, tri_upper)[0] - tied_f
    chosen = jnp.logical_or(above, jnp.logical_and(tied, tied_rank.reshape(n_e, nb, ROUTE_BLOCK) < need))
    chosen_f = chosen.astype(F32).reshape(rows, ROUTE_BLOCK)
    incl, before, total = _segment_count(chosen_f, seg_lower, tri_upper)
    slot_ref[...] = jnp.where(chosen_f > 0, incl - 1.0, -1.0).astype(I32)
    before_ref[...] = before.astype(I32)
    total_ref[...] = total.astype(I32)


def _route_select(aff3, cap):
    n_e, nb, _ = aff3.shape
    rows = n_e * nb
    table = jax.ShapeDtypeStruct((rows, ROUTE_BLOCK), I32)
    return pl.pallas_call(
        functools.partial(_route_select_kernel, cap=cap),
        out_shape=[table, table, table],
        compiler_params=_params(None, 48),
        name="route_select",
    )(aff3)


def _route_index_kernel(slot_ref, aff_ref, before_ref, ends_ref, totals_ref, idx_ref, gate_ref):
    cap = idx_ref.shape[1]
    nb = slot_ref.shape[0]
    s_col = lax.broadcasted_iota(I32, (cap, nb), 0)
    done = ends_ref[0] <= s_col
    blk = jnp.sum(done.astype(F32), axis=1, keepdims=True)
    blk_first = jnp.sum(jnp.where(done, totals_ref[0].astype(F32), 0.0), axis=1, keepdims=True)
    onehot = (lax.broadcasted_iota(I32, (cap, nb), 1) == blk.astype(I32)).astype(BF16)
    slot = slot_ref[...]
    local = jnp.where(slot >= 0, slot - before_ref[...] + 1, 0).astype(BF16)
    picked = _dot(onehot, local)
    target = s_col[:, :1].astype(F32) - blk_first + 1.0
    match = picked == target
    lane = lax.broadcasted_iota(I32, match.shape, 1).astype(F32)
    tok = blk * ROUTE_BLOCK + jnp.sum(jnp.where(match, lane, 0.0), axis=1, keepdims=True)
    idx_ref[0] = tok.astype(I32)
    a = aff_ref[...]
    hi = a.astype(BF16)
    r1 = a - hi.astype(F32)
    mid = r1.astype(BF16)
    lo = (r1 - mid.astype(F32)).astype(BF16)
    a_rows = _dot(onehot, hi) + _dot(onehot, mid) + _dot(onehot, lo)
    gate_ref[0] = jnp.sum(jnp.where(match, a_rows, 0.0), axis=1, keepdims=True)


def _route_index(slot, aff2, before, ends, totals, cap):
    rows = slot.shape[0]
    n_e = N_EXPERTS
    nb = rows // n_e
    blocks = pl.BlockSpec((nb, ROUTE_BLOCK), lambda e: (e, 0))
    row = pl.BlockSpec((1, 1, nb), lambda e: (e, 0, 0))
    out = pl.BlockSpec((1, cap, 1), lambda e: (e, 0, 0))
    return pl.pallas_call(
        _route_index_kernel,
        grid=(n_e,),
        in_specs=[blocks, blocks, blocks, row, row],
        out_specs=[out, out],
        out_shape=[jax.ShapeDtypeStruct((n_e, cap, 1), I32), jax.ShapeDtypeStruct((n_e, cap, 1), F32)],
        compiler_params=_params(("parallel",), 48),
        name="route_index",
    )(slot, aff2, before, ends, totals)


def _ffn_kernel(idx_ref, xn_hbm, gate_ref, wg_ref, wu_ref, wd_ref, ye_ref, stage_ref, xe_ref, acc_ref, sem):
    e = pl.program_id(0)
    f = pl.program_id(1)
    n_e = pl.num_programs(0)
    cap = xe_ref.shape[0]
    rows_per_step = cap // (EXPERT_FF // wg_ref.shape[2])
    buf = e % 2

    def row_copy(expert, slot, dst_buf):
        tok = idx_ref[expert * cap + slot]
        return pltpu.make_async_copy(xn_hbm.at[pl.ds(tok, 1), :], stage_ref.at[dst_buf, pl.ds(slot, 1), :],
                                     sem.at[dst_buf])

    def wait_buffer(which):
        pltpu.make_async_copy(xn_hbm.at[pl.ds(0, cap), :], stage_ref.at[which], sem.at[which]).wait()

    @pl.when(f == 0)
    def _():
        @pl.when(e == 0)
        def _():
            def start(slot, c):
                row_copy(0, slot, 0).start()
                return c

            lax.fori_loop(0, cap, start, 0, unroll=8)

        wait_buffer(buf)
        xe_ref[...] = stage_ref[buf].astype(BF16)
        acc_ref[...] = jnp.zeros_like(acc_ref)

    xe = xe_ref[...]
    g = _dot(xe, wg_ref[0].astype(BF16))
    u = _dot(xe, wu_ref[0].astype(BF16))
    h = (_silu(g) * u).astype(BF16)
    acc_ref[...] += _dot(h, wd_ref[0].astype(BF16))

    nxt = jnp.minimum(e + 1, n_e - 1)
    for r in range(rows_per_step):
        row_copy(nxt, f * rows_per_step + r, 1 - buf).start()

    @pl.when(f == pl.num_programs(1) - 1)
    def _():
        ye_ref[0] = (acc_ref[...] * gate_ref[0]).astype(BF16)

        @pl.when(e == n_e - 1)
        def _():
            wait_buffer(1 - buf)


def _ffn(idx, xn, gates3, w_gate, w_up, w_down):
    e, cap = idx.shape
    tf = 256
    grid_spec = pltpu.PrefetchScalarGridSpec(
        num_scalar_prefetch=1,
        grid=(e, EXPERT_FF // tf),
        in_specs=[
            pl.BlockSpec(memory_space=pl.ANY),
            pl.BlockSpec((1, cap, 1), lambda ee, f, ix: (ee, 0, 0)),
            pl.BlockSpec((1, D_MODEL, tf), lambda ee, f, ix: (ee, 0, f)),
            pl.BlockSpec((1, D_MODEL, tf), lambda ee, f, ix: (ee, 0, f)),
            pl.BlockSpec((1, tf, D_MODEL), lambda ee, f, ix: (ee, f, 0)),
        ],
        out_specs=pl.BlockSpec((1, cap, D_MODEL), lambda ee, f, ix: (ee, 0, 0)),
        scratch_shapes=[
            pltpu.VMEM((2, cap, D_MODEL), F32),
            pltpu.VMEM((cap, D_MODEL), BF16),
            pltpu.VMEM((cap, D_MODEL), F32),
            pltpu.SemaphoreType.DMA((2,)),
        ],
    )
    return pl.pallas_call(
        _ffn_kernel,
        grid_spec=grid_spec,
        out_shape=jax.ShapeDtypeStruct((e, cap, D_MODEL), BF16),
        compiler_params=_params(("arbitrary", "arbitrary"), 56),
        name="moe_ffn",
    )(idx.reshape(-1), xn, gates3, w_gate, w_up, w_down)


def _combine_kernel(lo_ref, cnt_ref, slot_ref, x1_ref, lo_row_ref, cnt_row_ref, lo_col_ref, cnt_col_ref, ye_hbm,
                    y_ref, stage_ref, sem):
    i = pl.program_id(0)
    nt = pl.num_programs(0)
    n_e = ye_hbm.shape[0]
    tt = slot_ref.shape[0]
    shift = BF16_ROWS.bit_length() - 1

    def plan(tile):
        entries = []
        base = jnp.int32(0)
        for e in range(n_e):
            lo = lo_ref[tile * n_e + e]
            cnt = cnt_ref[tile * n_e + e]
            first = lax.shift_right_logical(lo, shift)
            groups = jnp.where(cnt > 0, lax.shift_right_logical(lo + cnt + (BF16_ROWS - 1), shift) - first, 0)
            entries.append((first * BF16_ROWS, groups, base))
            base = base + groups * BF16_ROWS
        return entries, base

    def for_each_group(tile, buf, fn):
        entries, _ = plan(tile)
        for e, (a0, groups, base) in enumerate(entries):
            def body(j, c, e=e, a0=a0, base=base):
                src = ye_hbm.at[e, pl.ds(pl.multiple_of(a0 + j * BF16_ROWS, BF16_ROWS), BF16_ROWS), :]
                dst = stage_ref.at[buf, pl.ds(pl.multiple_of(base + j * BF16_ROWS, BF16_ROWS), BF16_ROWS), :]
                fn(pltpu.make_async_copy(src, dst, sem.at[buf]))
                return c

            lax.fori_loop(0, groups, body, 0)

    buf = i % 2

    @pl.when(i == 0)
    def _():
        stage_ref[...] = jnp.zeros_like(stage_ref)
        for_each_group(0, 0, lambda cp: cp.start())

    @pl.when(i + 1 < nt)
    def _():
        for_each_group(i + 1, 1 - buf, lambda cp: cp.start())

    for_each_group(i, buf, lambda cp: cp.wait())

    def staged(lo, cnt):
        first = lax.shift_right_logical(lo, shift)
        groups = jnp.where(cnt > 0, lax.shift_right_logical(lo + cnt + (BF16_ROWS - 1), shift) - first, 0)
        return first * BF16_ROWS, groups * BF16_ROWS

    a0_row, rows_row = staged(lo_row_ref[0], cnt_row_ref[0])
    _, rows_col = staged(lo_col_ref[0], cnt_col_ref[0])
    ei = lax.broadcasted_iota(I32, (n_e, n_e), 0)
    ej = lax.broadcasted_iota(I32, (n_e, n_e), 1)
    base_row = _dot(rows_row.astype(F32).astype(BF16), (ei < ej).astype(BF16)).astype(I32)
    base_col = _dot((ej < ei).astype(BF16),
                    jnp.broadcast_to(rows_col, (n_e, LANE)).astype(F32).astype(BF16))[:, :1].astype(I32)
    slot = slot_ref[...]
    rel = jnp.where(slot >= 0, slot - a0_row + base_row, -1)
    rel_hi = lax.shift_right_arithmetic(rel, 6).astype(F32).astype(BF16)
    rel_lo = jnp.bitwise_and(rel, 63).astype(F32).astype(BF16)
    _, total = plan(i)
    y_ref[...] = x1_ref[...]

    def k_block(kb, c):
        first = kb * COMBINE_K
        col_e = lax.broadcasted_iota(I32, (n_e, COMBINE_K), 1) + first
        owner = jnp.logical_and(col_e >= base_col, col_e < base_col + rows_col).astype(BF16)
        want = 64.0 * _dot(rel_hi, owner) + _dot(rel_lo, owner)
        col_t = (lax.broadcasted_iota(I32, (tt, COMBINE_K), 1) + first).astype(F32)
        rows = pl.ds(pl.multiple_of(first, COMBINE_K), COMBINE_K)
        y_ref[...] += _dot((want == col_t).astype(BF16), stage_ref[buf, rows, :])
        return c

    lax.fori_loop(0, lax.shift_right_logical(total + (COMBINE_K - 1), COMBINE_K.bit_length() - 1), k_block, 0)


def _combine(lo_flat, cnt_flat, slot, x1, ye):
    t = x1.shape[0]
    n_e = ye.shape[0]
    tt = ROUTE_BLOCK
    stage_rows = n_e * (tt + BF16_ROWS)
    assert stage_rows % COMBINE_K == 0
    nt = t // tt
    lo2 = lo_flat.reshape(nt, n_e)
    cnt2 = cnt_flat.reshape(nt, n_e)
    row_spec = pl.BlockSpec((1, 1, n_e), lambda i, lo, cnt: (i, 0, 0))
    col_spec = pl.BlockSpec((1, n_e, 1), lambda i, lo, cnt: (i, 0, 0))
    grid_spec = pltpu.PrefetchScalarGridSpec(
        num_scalar_prefetch=2,
        grid=(nt,),
        in_specs=[
            pl.BlockSpec((tt, n_e), lambda i, lo, cnt: (i, 0)),
            pl.BlockSpec((tt, D_MODEL), lambda i, lo, cnt: (i, 0)),
            row_spec, row_spec, col_spec, col_spec,
            pl.BlockSpec(memory_space=pl.ANY),
        ],
        out_specs=pl.BlockSpec((tt, D_MODEL), lambda i, lo, cnt: (i, 0)),
        scratch_shapes=[
            pltpu.VMEM((2, stage_rows, D_MODEL), BF16),
            pltpu.SemaphoreType.DMA((2,)),
        ],
    )
    return pl.pallas_call(
        _combine_kernel,
        grid_spec=grid_spec,
        out_shape=jax.ShapeDtypeStruct((t, D_MODEL), F32),
        compiler_params=_params(("arbitrary",), 48),
        name="moe_combine",
    )(lo_flat, cnt_flat, slot, x1, lo2[:, None, :], cnt2[:, None, :], lo2[:, :, None], cnt2[:, :, None], ye)


def _rope_tables(s):
    half = QK_ROPE // 2
    inv_freq = ROPE_THETA ** (-jnp.arange(half, dtype=F32) / half)
    ang = jnp.arange(s, dtype=F32)[:, None] * inv_freq[None, :]
    cos = jnp.cos(ang)
    sin = jnp.sin(ang)
    zeros = jnp.zeros((s, LANE - QK_ROPE), F32)
    return jnp.concatenate([cos, cos, zeros], axis=1), jnp.concatenate([-sin, sin, zeros], axis=1)


def _prepare_weights(norm1_g, w_in, q_a_norm_g, w_q_b, kv_a_norm_g, w_kv_b, q_norm_g, k_norm_g, conv_w,
                     a_log_fwd, a_log_bwd, dt_bias_fwd, dt_bias_bwd, gdn_norm_g, w_o, norm2_g, w_router):
    o_kv = Q_LORA
    o_kr = o_kv + KV_LORA
    o_qkv = o_kr + QK_ROPE
    o_z = o_qkv + QKV_GDN
    o_g = o_z + GDN_WIDTH
    wmisc = jnp.concatenate(
        [w_in[:, o_kr:o_qkv], w_in[:, o_g:], jnp.zeros((D_MODEL, LANE - QK_ROPE - 4 * GDN_HEADS), F32)],
        axis=1).astype(BF16)
    wq = jnp.pad(w_q_b.reshape(Q_LORA, MLA_HEADS, QK_HEAD), ((0, 0), (0, 0), (0, QK_PAD - QK_HEAD)))
    wq = wq.reshape(Q_LORA, MLA_HEADS * QK_PAD).astype(BF16)
    pad_g = lambda g: jnp.pad(g, (0, QK_PAD - QK_HEAD)).reshape(1, QK_PAD)
    alog = jnp.concatenate([a_log_fwd, a_log_bwd]).reshape(2 * GDN_HEADS, 1, 1)
    dtb = jnp.concatenate([dt_bias_fwd, dt_bias_bwd]).reshape(2 * GDN_HEADS, 1, 1)
    return dict(
        g1=norm1_g.reshape(1, D_MODEL), wz=w_in[:, o_z:o_g].astype(BF16), wlat=w_in[:, :o_kr].astype(BF16),
        wqkv=w_in[:, o_qkv:o_z].astype(BF16), wmisc=wmisc,
        gqa=q_a_norm_g.reshape(1, Q_LORA), gkva=kv_a_norm_g.reshape(1, KV_LORA),
        wq=wq, wkv=w_kv_b.astype(BF16), gq=pad_g(q_norm_g), gk=pad_g(k_norm_g),
        conv_w=conv_w,
        alog=jnp.broadcast_to(alog, (2 * GDN_HEADS, 1, LANE)), dtb=jnp.broadcast_to(dtb, (2 * GDN_HEADS, 1, LANE)),
        gn=gdn_norm_g.reshape(1, GDN_DV), wo=w_o.astype(BF16), g2=norm2_g.reshape(1, D_MODEL), wr=w_router,
    )


def _route(aff, cap):
    t, n_e = aff.shape
    nb = t // ROUTE_BLOCK
    aff3 = aff.T.reshape(n_e, nb, ROUTE_BLOCK)
    slot, before, total = _route_select(aff3, cap)
    before_b = before[:, 0].reshape(n_e, nb)
    total_b = total[:, 0].reshape(n_e, nb)
    idx, gates = _route_index(slot, aff3.reshape(n_e * nb, ROUTE_BLOCK), before,
                              (before_b + total_b).reshape(n_e, 1, nb), total_b.reshape(n_e, 1, nb), cap)
    slot_by_token = slot.reshape(n_e, t).T
    return idx.reshape(n_e, cap), gates, slot_by_token, before_b.T.reshape(-1), total_b.T.reshape(-1)


def _encoder_layer(x, p, w_gate, w_up, w_down):
    bt, s, _ = x.shape
    t = bt * s
    x2d = x.reshape(t, D_MODEL)
    oa, ob = _in_proj(x2d, p["g1"], p["wz"], p["wlat"], p["wqkv"], p["wmisc"])

    cos, sin = _rope_tables(s)
    q, k, v = _mla_prep(oa, ob, cos, sin, p["gqa"], p["gkva"], p["wq"], p["wkv"], p["gq"], p["gk"], bt, s)
    mla_out = _attention(q, k, v)

    ob3 = ob.reshape(bt, s, NB)
    qkv_h = _conv(ob3, p["conv_w"])
    g_raw = ob3[:, :, QKV_GDN + QK_ROPE:QKV_GDN + QK_ROPE + 4 * GDN_HEADS]
    g_t = jnp.swapaxes(g_raw, 1, 2).reshape(bt, 4 * GDN_HEADS, s // LANE, LANE)
    g_out = _gates(g_t, p["alog"], p["dtb"]).reshape(bt, 4, GDN_HEADS, s)
    g_rows = g_out[:, :2].reshape(bt, 2 * GDN_HEADS, s // CHUNK, CHUNK)
    g_cols = jnp.transpose(g_out, (0, 2, 3, 1))
    o_fwd, o_bwd = _gdn_scan(_gdn_prep(qkv_h, g_rows, g_cols), g_rows)

    x1, xn, aff = _out_proj(x2d, mla_out.reshape(t, MLA_WIDTH), o_fwd.reshape(t, GDN_WIDTH),
                            o_bwd.reshape(t, GDN_WIDTH), oa, p["gn"], p["wo"], p["g2"], p["wr"])

    cap = max(1, CAPACITY_FACTOR * t // N_EXPERTS)
    idx, gates, slot, lo, cnt = _route(aff, cap)
    ye = _ffn(idx, xn, gates, w_gate, w_up, w_down)
    y = _combine(lo, cnt, slot, x1, ye)
    return y.reshape(bt, s, D_MODEL)


def kernel(x_prompt, x_sample, norm1_g, w_in, q_a_norm_g, w_q_b, kv_a_norm_g, w_kv_b, q_norm_g, k_norm_g, conv_w,
           a_log_fwd, a_log_bwd, dt_bias_fwd, dt_bias_bwd, gdn_norm_g, w_o, norm2_g, w_router, w_gate, w_up, w_down):
    y_prompt = x_prompt
    y_sample = x_sample
    for l in range(norm1_g.shape[0]):
        p = _prepare_weights(norm1_g[l], w_in[l], q_a_norm_g[l], w_q_b[l], kv_a_norm_g[l], w_kv_b[l], q_norm_g[l],
                             k_norm_g[l], conv_w[l], a_log_fwd[l], a_log_bwd[l], dt_bias_fwd[l], dt_bias_bwd[l],
                             gdn_norm_g[l], w_o[l], norm2_g[l], w_router[l])
        y_prompt = _encoder_layer(y_prompt, p, w_gate[l], w_up[l], w_down[l])
        y_sample = _encoder_layer(y_sample, p, w_gate[l], w_up[l], w_down[l])
    return (y_prompt, y_sample)
```

```python
import functools
import math

import jax
import jax.numpy as jnp
from jax import lax
from jax.experimental import pallas as pl
from jax.experimental.pallas import tpu as pltpu

F32 = jnp.float32
BF16 = jnp.bfloat16
I32 = jnp.int32

D_MODEL = 2048
MLA_HEADS = 8
Q_LORA = 512
KV_LORA = 256
QK_NOPE = 128
QK_ROPE = 64
QK_HEAD = QK_NOPE + QK_ROPE
QK_PAD = 256
V_HEAD = 128
V_PAD = 256
ROPE_THETA = 10000.0
GDN_HEADS = 8
GDN_DK = 128
GDN_DV = 128
QKV_GDN = 2 * GDN_HEADS * GDN_DK + GDN_HEADS * GDN_DV
GDN_WIDTH = GDN_HEADS * GDN_DV
MLA_WIDTH = MLA_HEADS * V_HEAD
CONV_W = 5
CHUNK = 64
N_EXPERTS = 16
EXPERT_FF = 1024
CAPACITY_FACTOR = 2
EPS = 1e-6

LANE = 128
SUBLANES = 8
DMA_THREADS = 2
BF16_ROWS = 16
MIB = 1024 * 1024

NA = GDN_WIDTH + Q_LORA + KV_LORA
NB = QKV_GDN + LANE
QLAT_COL_BLOCK = GDN_WIDTH // Q_LORA
KVLAT_COL_BLOCK = (GDN_WIDTH + Q_LORA) // KV_LORA
MISC_COL_BLOCK = QKV_GDN // LANE
PREP_CHUNKS = 16
SCAN_CHUNKS = 8
ATTN_SUB = 256
ROUTE_BLOCK = LANE
COMBINE_K = 256
EXP_SPAN = 126
MANTISSA_STEPS = 40

NN = (((1,), (0,)), ((), ()))
NT = (((1,), (1,)), ((), ()))
TN = (((0,), (0,)), ((), ()))


def _params(semantics, vmem_mib):
    return pltpu.CompilerParams(dimension_semantics=semantics, vmem_limit_bytes=vmem_mib * MIB)


def _resident(shape):
    nd = len(shape)
    return pl.BlockSpec(shape, lambda *_: (0,) * nd, pipeline_mode=pl.Buffered(1))


def _dot(a, b, dims=NN):
    return lax.dot_general(a, b, dims, preferred_element_type=F32)


def _dotb(a, b, dims=NN):
    return lax.dot_general(a.astype(BF16), b.astype(BF16), dims, preferred_element_type=F32)


def _split(x):
    hi = x.astype(BF16)
    lo = (x - hi.astype(F32)).astype(BF16)
    return hi, lo


def _dot3(a, b, dims=NN):
    ah, al = _split(a)
    bh, bl = _split(b)
    return _dot(ah, bh, dims) + (_dot(ah, bl, dims) + _dot(al, bh, dims))


def _silu(x):
    return x * jax.nn.sigmoid(x)


def _in_proj_kernel(x_ref, g_ref, wz_ref, wlat_ref, wqkv_ref, wmisc_ref, oa_ref, ob_ref):
    x = x_ref[...]
    ms = jnp.mean(x * x, axis=-1, keepdims=True)
    h = (x * lax.rsqrt(ms + EPS) * g_ref[...]).astype(BF16)
    oa_ref[:, :GDN_WIDTH] = _dot(h, wz_ref[...]).astype(BF16)
    oa_ref[:, GDN_WIDTH:] = _dot(h, wlat_ref[...]).astype(BF16)
    ob_ref[:, :QKV_GDN] = _dot(h, wqkv_ref[...])
    ob_ref[:, QKV_GDN:] = _dot(h, wmisc_ref[...])


def _in_proj(x2d, g, wz, wlat, wqkv, wmisc):
    t = x2d.shape[0]
    tm = min(512, t)
    return pl.pallas_call(
        _in_proj_kernel,
        grid=(t // tm,),
        in_specs=[
            pl.BlockSpec((tm, D_MODEL), lambda i: (i, 0)),
            _resident((1, D_MODEL)),
            _resident(wz.shape),
            _resident(wlat.shape),
            _resident(wqkv.shape),
            _resident(wmisc.shape),
        ],
        out_specs=[
            pl.BlockSpec((tm, NA), lambda i: (i, 0)),
            pl.BlockSpec((tm, NB), lambda i: (i, 0)),
        ],
        out_shape=[
            jax.ShapeDtypeStruct((t, NA), BF16),
            jax.ShapeDtypeStruct((t, NB), F32),
        ],
        compiler_params=_params(("parallel",), 56),
        name="in_proj",
    )(x2d, g, wz, wlat, wqkv, wmisc)


def _mla_prep_kernel(ql_ref, kvl_ref, misc_ref, cos_ref, sin_ref, gqa_ref, gkva_ref, wq_ref, wkv_ref,
                     gq_ref, gk_ref, q_ref, k_ref, v_ref):
    def norm(x, g):
        return x * lax.rsqrt(jnp.mean(x * x, axis=-1, keepdims=True) + EPS) * g

    qn = norm(ql_ref[...].astype(F32), gqa_ref[...]).astype(BF16)
    kvn = norm(kvl_ref[...].astype(F32), gkva_ref[...]).astype(BF16)
    q = _dot(qn, wq_ref[...])
    kv = _dot(kvn, wkv_ref[...])

    cos = cos_ref[...]
    sin = sin_ref[...]
    lane = lax.broadcasted_iota(I32, cos.shape, 1)

    def rope(t):
        rot = jnp.where(lane < QK_ROPE // 2, pltpu.roll(t, LANE - QK_ROPE // 2, 1), pltpu.roll(t, QK_ROPE // 2, 1))
        return t * cos + rot * sin

    gq = gq_ref[...]
    gk = gk_ref[...]
    kr = jnp.where(lane < QK_ROPE, misc_ref[...], 0.0)
    kr_ssq = jnp.sum(kr * kr, axis=-1, keepdims=True)
    kr_rot = rope(kr * gk[:, QK_NOPE:])
    q_scale = QK_HEAD ** -0.5
    ones_col = jnp.where(lane == 0, 1.0, 0.0).astype(BF16)
    for h in range(MLA_HEADS):
        qh = q[:, h * QK_PAD:(h + 1) * QK_PAD]
        sc = lax.rsqrt(jnp.sum(qh * qh, axis=-1, keepdims=True) * (1.0 / QK_HEAD) + EPS) * q_scale
        q_ref[0, h, :, :QK_NOPE] = (qh[:, :QK_NOPE] * gq[:, :QK_NOPE] * sc).astype(BF16)
        q_ref[0, h, :, QK_NOPE:] = (rope(qh[:, QK_NOPE:] * gq[:, QK_NOPE:]) * sc).astype(BF16)
        kn = kv[:, h * 256:h * 256 + QK_NOPE]
        sk = lax.rsqrt((jnp.sum(kn * kn, axis=-1, keepdims=True) + kr_ssq) * (1.0 / QK_HEAD) + EPS)
        k_ref[0, h, :, :QK_NOPE] = (kn * gk[:, :QK_NOPE] * sk).astype(BF16)
        k_ref[0, h, :, QK_NOPE:] = (kr_rot * sk).astype(BF16)
        v_ref[0, h, :, :V_HEAD] = kv[:, h * 256 + QK_NOPE:(h + 1) * 256].astype(BF16)
        v_ref[0, h, :, V_HEAD:] = ones_col


def _mla_prep(oa, ob, cos, sin, gqa, gkva, wq, wkv, gq, gk, bt, s):
    tm = min(512, s)
    nt = s // tm
    return pl.pallas_call(
        _mla_prep_kernel,
        grid=(bt, nt),
        in_specs=[
            pl.BlockSpec((tm, Q_LORA), lambda b, i: (b * nt + i, QLAT_COL_BLOCK)),
            pl.BlockSpec((tm, KV_LORA), lambda b, i: (b * nt + i, KVLAT_COL_BLOCK)),
            pl.BlockSpec((tm, LANE), lambda b, i: (b * nt + i, MISC_COL_BLOCK)),
            pl.BlockSpec((tm, LANE), lambda b, i: (i, 0)),
            pl.BlockSpec((tm, LANE), lambda b, i: (i, 0)),
            _resident(gqa.shape),
            _resident(gkva.shape),
            _resident(wq.shape),
            _resident(wkv.shape),
            _resident(gq.shape),
            _resident(gk.shape),
        ],
        out_specs=[
            pl.BlockSpec((1, MLA_HEADS, tm, QK_PAD), lambda b, i: (b, 0, i, 0)),
            pl.BlockSpec((1, MLA_HEADS, tm, QK_PAD), lambda b, i: (b, 0, i, 0)),
            pl.BlockSpec((1, MLA_HEADS, tm, V_PAD), lambda b, i: (b, 0, i, 0)),
        ],
        out_shape=[
            jax.ShapeDtypeStruct((bt, MLA_HEADS, s, QK_PAD), BF16),
            jax.ShapeDtypeStruct((bt, MLA_HEADS, s, QK_PAD), BF16),
            jax.ShapeDtypeStruct((bt, MLA_HEADS, s, V_PAD), BF16),
        ],
        compiler_params=_params(("parallel", "parallel"), 48),
        name="mla_prep",
    )(oa, oa, ob, cos, sin, gqa, gkva, wq, wkv, gq, gk)


def _attn_kernel(q_ref, k_ref, v_ref, o_ref):
    k = k_ref[0, 0]
    v = v_ref[0, 0]
    parts = [slice(r, r + ATTN_SUB) for r in range(0, q_ref.shape[2], ATTN_SUB)]
    scores = [_dot(q_ref[0, 0, rows, :], k, NT) for rows in parts]
    for rows, s in zip(parts, scores):
        m = jnp.max(s, axis=-1, keepdims=True)
        p = jnp.exp((s - m).astype(BF16))
        o = _dot(p, v)
        o_ref[0, rows, :] = (o[:, :V_HEAD] / o[:, V_HEAD:V_HEAD + 1]).astype(BF16)


def _attention(q, k, v):
    bt, h, s, _ = q.shape
    tq = min(4 * ATTN_SUB, s)
    return pl.pallas_call(
        _attn_kernel,
        grid=(bt, h, s // tq),
        in_specs=[
            pl.BlockSpec((1, 1, tq, QK_PAD), lambda b, hh, i: (b, hh, i, 0)),
            pl.BlockSpec((1, 1, s, QK_PAD), lambda b, hh, i: (b, hh, 0, 0)),
            pl.BlockSpec((1, 1, s, V_PAD), lambda b, hh, i: (b, hh, 0, 0)),
        ],
        out_specs=pl.BlockSpec((1, tq, V_HEAD), lambda b, hh, i: (b, i, hh)),
        out_shape=jax.ShapeDtypeStruct((bt, s, MLA_WIDTH), BF16),
        compiler_params=_params(("parallel", "parallel", "parallel"), 48),
        name="attention",
    )(q, k, v)


def _conv_kernel(x_ref, w_ref, o_ref, pad_ref):
    c = pl.program_id(1)
    x = x_ref[0]
    s = x.shape[0]
    w = w_ref[...]
    pad = CONV_W // 2
    halo = jnp.zeros((SUBLANES, LANE), F32)
    pad_ref[:SUBLANES, :] = halo
    pad_ref[SUBLANES + s:, :] = halo
    pad_ref[SUBLANES:SUBLANES + s, :] = x
    acc = x * w[pad:pad + 1]
    for d in range(-pad, pad + 1):
        if d != 0:
            acc = acc + pad_ref[SUBLANES + d:SUBLANES + d + s, :] * w[pad + d:pad + d + 1]
    y = _silu(acc)
    inv = lax.rsqrt(jnp.sum(y * y, axis=-1, keepdims=True) + EPS)
    is_q = c < GDN_HEADS
    is_qk = c < 2 * GDN_HEADS
    o_ref[0, 0] = y * (jnp.where(is_qk, inv, 1.0) * jnp.where(is_q, GDN_DK ** -0.5, 1.0))


def _conv(ob3, conv_w):
    bt, s, _ = ob3.shape
    nblk = QKV_GDN // LANE
    return pl.pallas_call(
        _conv_kernel,
        grid=(bt, nblk),
        in_specs=[
            pl.BlockSpec((1, s, LANE), lambda b, c: (b, 0, c)),
            pl.BlockSpec((CONV_W, LANE), lambda b, c: (0, c)),
        ],
        out_specs=pl.BlockSpec((1, 1, s, LANE), lambda b, c: (b, c, 0, 0)),
        out_shape=jax.ShapeDtypeStruct((bt, nblk, s, LANE), F32),
        scratch_shapes=[pltpu.VMEM((s + 2 * SUBLANES, LANE), F32)],
        compiler_params=_params(("parallel", "parallel"), 48),
        name="gdn_conv",
    )(ob3, conv_w)


def _gates_kernel(x_ref, alog_ref, dtb_ref, o_ref):
    x = x_ref[0]
    nh = GDN_HEADS
    a = x[:2 * nh] + dtb_ref[...]
    softplus = jnp.maximum(a, 0.0) + jnp.log1p(jnp.exp(-jnp.abs(a)))
    g = -jnp.exp(alog_ref[...]) * softplus
    beta = jax.nn.sigmoid(x[2 * nh:])
    gf = g[:nh]
    gb = g[nh:]
    pos = lax.broadcasted_iota(I32, gf.shape, 2) % CHUNK
    sh = 1
    while sh < CHUNK:
        gf = gf + jnp.where(pos >= sh, pltpu.roll(gf, sh, 2), 0.0)
        gb = gb + jnp.where(pos < CHUNK - sh, pltpu.roll(gb, LANE - sh, 2), 0.0)
        sh *= 2
    o_ref[0, :nh] = gf
    o_ref[0, nh:2 * nh] = gb
    o_ref[0, 2 * nh:] = beta


def _gates(gt, alog, dtb):
    bt, r, sl, _ = gt.shape
    return pl.pallas_call(
        _gates_kernel,
        grid=(bt,),
        in_specs=[
            pl.BlockSpec((1, r, sl, LANE), lambda b: (b, 0, 0, 0)),
            _resident(alog.shape),
            _resident(dtb.shape),
        ],
        out_specs=pl.BlockSpec((1, r, sl, LANE), lambda b: (b, 0, 0, 0)),
        out_shape=jax.ShapeDtypeStruct(gt.shape, F32),
        compiler_params=_params(("parallel",), 32),
        name="gdn_gates",
    )(gt, alog, dtb)


def _unit_tri_inverses(lmats, eye):
    ps = [eye - l for l in lmats]
    ms = [_dotb(l, l) for l in lmats]
    n = 2
    while n < CHUNK // 2:
        ps = [p + _dotb(p, m) for p, m in zip(ps, ms)]
        ms = [_dotb(m, m) for m in ms]
        n *= 2
    return [p + _dotb(p, m) for p, m in zip(ps, ms)]


def _gdn_prep_kernel(q_ref, k_ref, v_ref, grf_ref, grb_ref, gc_ref,
                     wqf_ref, uf_ref, kdf_ref, inf_ref, wqb_ref, ub_ref, kdb_ref, inb_ref):
    ri = lax.broadcasted_iota(I32, (CHUNK, CHUNK), 0)
    ci = lax.broadcasted_iota(I32, (CHUNK, CHUNK), 1)
    eye = (ri == ci).astype(F32)
    dirs = (
        (0, grf_ref, ri >= ci, ri > ci, CHUNK - 1, (wqf_ref, uf_ref, kdf_ref, inf_ref)),
        (1, grb_ref, ri <= ci, ri < ci, 0, (wqb_ref, ub_ref, kdb_ref, inb_ref)),
    )
    lmats = []
    work = []
    for c in range(PREP_CHUNKS):
        rows = slice(c * CHUNK, (c + 1) * CHUNK)
        q = q_ref[0, 0, rows, :]
        k = k_ref[0, 0, rows, :]
        v = v_ref[0, 0, rows, :]
        g4 = gc_ref[0, 0, rows, :]
        kk = _dotb(k, k, NT)
        qk = _dotb(q, k, NT)
        for col, grow_ref, incl, strict, last_row, outs in dirs:
            gcol = g4[:, col:col + 1]
            beta = g4[:, 2 + col:3 + col]
            decay = jnp.exp(jnp.where(incl, gcol - grow_ref[0, 0, c:c + 1, :], -jnp.inf))
            lmats.append(jnp.where(strict, beta * kk * decay, 0.0))
            e_gc = jnp.exp(gcol)
            rhs = jnp.concatenate([v * beta, k * (beta * e_gc)], axis=-1)
            k_dec = k * jnp.exp(g4[last_row:last_row + 1, col:col + 1] - gcol)
            intra = jnp.where(incl, qk * decay, 0.0)
            work.append((c, rows, rhs, q * e_gc, k_dec, intra, outs))
    tinvs = _unit_tri_inverses(lmats, eye)
    for tinv, (c, rows, rhs, q_dec, k_dec, intra, (wq_ref, u_ref, kd_ref, in_ref)) in zip(tinvs, work):
        sol = _dotb(tinv, rhs)
        u_ref[0, 0, rows, :] = sol[:, :GDN_DV].astype(BF16)
        wq_ref[0, 0, c, :CHUNK, :] = sol[:, GDN_DV:].astype(BF16)
        wq_ref[0, 0, c, CHUNK:, :] = q_dec.astype(BF16)
        kd_ref[0, 0, rows, :] = k_dec.astype(BF16)
        in_ref[0, 0, rows, :] = intra.astype(BF16)


def _gdn_prep(qkv_h, g_rows, g_cols):
    bt, _, s, _ = qkv_h.shape
    nh = GDN_HEADS
    rb = PREP_CHUNKS * CHUNK
    n_chunks = s // CHUNK
    head = lambda off: (lambda b, h, i: (b, off + h, i, 0))
    seq_spec = lambda w: pl.BlockSpec((1, 1, rb, w), head(0))
    wq_spec = pl.BlockSpec((1, 1, PREP_CHUNKS, 2 * CHUNK, GDN_DK), lambda b, h, i: (b, h, i, 0, 0))
    out_specs = [wq_spec, seq_spec(GDN_DV), seq_spec(GDN_DK), seq_spec(CHUNK)]
    out_shape = [
        jax.ShapeDtypeStruct((bt, nh, n_chunks, 2 * CHUNK, GDN_DK), BF16),
        jax.ShapeDtypeStruct((bt, nh, s, GDN_DV), BF16),
        jax.ShapeDtypeStruct((bt, nh, s, GDN_DK), BF16),
        jax.ShapeDtypeStruct((bt, nh, s, CHUNK), BF16),
    ]
    return pl.pallas_call(
        _gdn_prep_kernel,
        grid=(bt, nh, s // rb),
        in_specs=[
            pl.BlockSpec((1, 1, rb, GDN_DK), head(0)),
            pl.BlockSpec((1, 1, rb, GDN_DK), head(nh)),
            pl.BlockSpec((1, 1, rb, GDN_DV), head(2 * nh)),
            pl.BlockSpec((1, 1, PREP_CHUNKS, CHUNK), head(0)),
            pl.BlockSpec((1, 1, PREP_CHUNKS, CHUNK), head(nh)),
            pl.BlockSpec((1, 1, rb, 4), head(0)),
        ],
        out_specs=out_specs + out_specs,
        out_shape=out_shape + out_shape,
        compiler_params=_params(("parallel", "parallel", "parallel"), 48),
        name="gdn_prep",
    )(qkv_h, qkv_h, qkv_h, g_rows, g_rows, g_cols)


def _gdn_scan_kernel(wqf_ref, uf_ref, kdf_ref, inf_ref, grf_ref, wqb_ref, ub_ref, kdb_ref, inb_ref, grb_ref,
                     of_ref, ob_ref, state_ref):
    @pl.when(pl.program_id(1) == 0)
    def _():
        state_ref[...] = jnp.zeros_like(state_ref)

    def step(c, carry):
        dirs = (
            (0, c, wqf_ref, uf_ref, kdf_ref, inf_ref, grf_ref, CHUNK - 1, of_ref),
            (1, SCAN_CHUNKS - 1 - c, wqb_ref, ub_ref, kdb_ref, inb_ref, grb_ref, 0, ob_ref),
        )
        chains = [(h,) + d for h in range(GDN_HEADS) for d in dirs]
        states = [state_ref[d, h] for h, d, *_ in chains]
        rs = [_dot(wq_ref[0, h, cc], s.astype(BF16)) for (h, _, cc, wq_ref, *_), s in zip(chains, states)]
        for (h, d, cc, _, u_ref, kd_ref, in_ref, g_ref, last_lane, o_ref), s, r in zip(chains, states, rs):
            rows = pl.ds(pl.multiple_of(cc * CHUNK, CHUNK), CHUNK)
            v_new = (u_ref[0, h, rows, :].astype(F32) - r[:CHUNK]).astype(BF16)
            o = r[CHUNK:] + _dot(in_ref[0, h, rows, :], v_new)
            o_ref[0, rows, h * GDN_DV:(h + 1) * GDN_DV] = o.astype(BF16)
            g_last = g_ref[0, d * GDN_HEADS + h, pl.ds(cc, 1), :][:, last_lane:last_lane + 1]
            state_ref[d, h] = s * jnp.exp(g_last) + _dot(kd_ref[0, h, rows, :], v_new, TN)
        return carry

    lax.fori_loop(0, SCAN_CHUNKS, step, 0)


def _gdn_scan(prep, g_rows):
    wqf, uf, kdf, inf_, wqb, ub, kdb, inb = prep
    bt, nh, s, _ = uf.shape
    rb = SCAN_CHUNKS * CHUNK
    nblk = s // rb
    fwd = lambda b, i: (b, 0, i, 0)
    bwd = lambda b, i: (b, 0, nblk - 1 - i, 0)

    def specs(idx):
        idx5 = lambda b, i: idx(b, i) + (0,)
        return [
            pl.BlockSpec((1, nh, SCAN_CHUNKS, 2 * CHUNK, GDN_DK), idx5),
            pl.BlockSpec((1, nh, rb, GDN_DV), idx),
            pl.BlockSpec((1, nh, rb, GDN_DK), idx),
            pl.BlockSpec((1, nh, rb, CHUNK), idx),
            pl.BlockSpec((1, 2 * nh, SCAN_CHUNKS, CHUNK), idx),
        ]

    out = jax.ShapeDtypeStruct((bt, s, GDN_WIDTH), BF16)
    return pl.pallas_call(
        _gdn_scan_kernel,
        grid=(bt, nblk),
        in_specs=specs(fwd) + specs(bwd),
        out_specs=[
            pl.BlockSpec((1, rb, GDN_WIDTH), lambda b, i: (b, i, 0)),
            pl.BlockSpec((1, rb, GDN_WIDTH), lambda b, i: (b, nblk - 1 - i, 0)),
        ],
        out_shape=[out, out],
        scratch_shapes=[pltpu.VMEM((2, nh, GDN_DK, GDN_DV), F32)],
        compiler_params=_params(("parallel", "arbitrary"), 48),
        name="gdn_scan",
    )(wqf, uf, kdf, inf_, g_rows, wqb, ub, kdb, inb, g_rows)


def _out_proj_kernel(x_ref, mla_ref, of_ref, ob_ref, z_ref, gn_ref, wo_ref, g_ref, wr_ref, x1_ref, xn_ref, aff_ref,
                     gdn_ref):
    for h in range(GDN_HEADS):
        cols = slice(h * GDN_DV, (h + 1) * GDN_DV)
        o = of_ref[:, cols].astype(F32) + ob_ref[:, cols].astype(F32)
        on = o * lax.rsqrt(jnp.mean(o * o, axis=-1, keepdims=True) + EPS) * gn_ref[...]
        gdn_ref[:, cols] = (on * _silu(z_ref[:, cols].astype(F32))).astype(BF16)
    y = x_ref[...] + _dot(mla_ref[...], wo_ref[:MLA_WIDTH, :]) + _dot(gdn_ref[...], wo_ref[MLA_WIDTH:, :])
    x1_ref[...] = y
    xn = y * lax.rsqrt(jnp.mean(y * y, axis=-1, keepdims=True) + EPS) * g_ref[...]
    xn_ref[...] = xn
    logits = _dot3(xn, wr_ref[...])
    e = jnp.exp(logits - jnp.max(logits, axis=-1, keepdims=True))
    aff_ref[...] = e / jnp.sum(e, axis=-1, keepdims=True)


def _out_proj(x2d, mla, o_fwd, o_bwd, oa, gn, wo, g2, wr):
    t = x2d.shape[0]
    tm = min(256, t)
    rows = lambda w: pl.BlockSpec((tm, w), lambda i: (i, 0))
    return pl.pallas_call(
        _out_proj_kernel,
        grid=(t // tm,),
        in_specs=[rows(D_MODEL), rows(MLA_WIDTH), rows(GDN_WIDTH), rows(GDN_WIDTH), rows(GDN_WIDTH),
                  _resident(gn.shape), _resident(wo.shape), _resident(g2.shape), _resident(wr.shape)],
        out_specs=[rows(D_MODEL), rows(D_MODEL), rows(N_EXPERTS)],
        out_shape=[
            jax.ShapeDtypeStruct((t, D_MODEL), F32),
            jax.ShapeDtypeStruct((t, D_MODEL), F32),
            jax.ShapeDtypeStruct((t, N_EXPERTS), F32),
        ],
        scratch_shapes=[pltpu.VMEM((tm, GDN_WIDTH), BF16)],
        compiler_params=_params(("parallel",), 48),
        name="out_proj",
    )(x2d, mla, o_fwd, o_bwd, oa, gn, wo, g2, wr)


def _segment_count(flags, seg_lower, tri_upper):
    within = _dot(flags.astype(BF16), tri_upper)
    total = jnp.broadcast_to(within[:, ROUTE_BLOCK - 1:], within.shape)
    before = _dot(seg_lower, total.astype(BF16))
    return within + before, before, total


def _route_select_kernel(aff_ref, slot_ref, before_ref, total_ref, *, cap):
    n_e, nb, _ = aff_ref.shape
    rows = n_e * nb
    aff = aff_ref[...]

    def count_ge(t):
        return jnp.sum((aff >= t).astype(F32), axis=(1, 2), keepdims=True)

    e_hi = jnp.full((n_e, 1, 1), -1.0, F32)
    e_lo = jnp.full((n_e, 1, 1), float(EXP_SPAN), F32)
    for _ in range(EXP_SPAN.bit_length()):
        e_mid = jnp.floor((e_hi + e_lo) * 0.5)
        enough = count_ge(jnp.exp2(-e_mid)) >= cap
        e_lo = jnp.where(enough, e_mid, e_lo)
        e_hi = jnp.where(enough, e_hi, e_mid)
    lo = jnp.where(e_lo >= EXP_SPAN, 0.0, jnp.exp2(-e_lo))
    hi = jnp.exp2(-e_hi)
    for _ in range(MANTISSA_STEPS):
        mid = lo + (hi - lo) * 0.5
        enough = count_ge(mid) >= cap
        lo = jnp.where(enough, mid, lo)
        hi = jnp.where(enough, hi, mid)
    above = aff >= hi
    tied = jnp.logical_and(aff >= lo, jnp.logical_not(above))
    need = cap - jnp.sum(above.astype(F32), axis=(1, 2), keepdims=True)

    ri = lax.broadcasted_iota(I32, (rows, rows), 0)
    ci = lax.broadcasted_iota(I32, (rows, rows), 1)
    nb_shift = nb.bit_length() - 1
    assert nb == 1 << nb_shift
    same_expert = lax.shift_right_logical(ri, nb_shift) == lax.shift_right_logical(ci, nb_shift)
    seg_lower = jnp.logical_and(same_expert, ci < ri).astype(BF16)
    tri_upper = (lax.broadcasted_iota(I32, (ROUTE_BLOCK, ROUTE_BLOCK), 0)
                 <= lax.broadcasted_iota(I32, (ROUTE_BLOCK, ROUTE_BLOCK), 1)).astype(BF16)

    tied_f = tied.astype(F32).reshape(rows, ROUTE_BLOCK)
    tied_rank = _segment_count(tied_f, seg_lower, tri_upper)[0] - tied_f
    chosen = jnp.logical_or(above, jnp.logical_and(tied, tied_rank.reshape(n_e, nb, ROUTE_BLOCK) < need))
    chosen_f = chosen.astype(F32).reshape(rows, ROUTE_BLOCK)
    incl, before, total = _segment_count(chosen_f, seg_lower, tri_upper)
    slot_ref[...] = jnp.where(chosen_f > 0, incl - 1.0, -1.0).astype(I32)
    before_ref[...] = before.astype(I32)
    total_ref[...] = total.astype(I32)


def _route_select(aff3, cap):
    n_e, nb, _ = aff3.shape
    rows = n_e * nb
    table = jax.ShapeDtypeStruct((rows, ROUTE_BLOCK), I32)
    return pl.pallas_call(
        functools.partial(_route_select_kernel, cap=cap),
        out_shape=[table, table, table],
        compiler_params=_params(None, 48),
        name="route_select",
    )(aff3)


def _route_index_kernel(slot_ref, aff_ref, before_ref, ends_ref, totals_ref, idx_ref, gate_ref):
    cap = idx_ref.shape[1]
    nb = slot_ref.shape[0]
    s_col = lax.broadcasted_iota(I32, (cap, nb), 0)
    done = ends_ref[0] <= s_col
    blk = jnp.sum(done.astype(F32), axis=1, keepdims=True)
    blk_first = jnp.sum(jnp.where(done, totals_ref[0].astype(F32), 0.0), axis=1, keepdims=True)
    onehot = (lax.broadcasted_iota(I32, (cap, nb), 1) == blk.astype(I32)).astype(BF16)
    slot = slot_ref[...]
    local = jnp.where(slot >= 0, slot - before_ref[...] + 1, 0).astype(BF16)
    picked = _dot(onehot, local)
    target = s_col[:, :1].astype(F32) - blk_first + 1.0
    match = picked == target
    lane = lax.broadcasted_iota(I32, match.shape, 1).astype(F32)
    tok = blk * ROUTE_BLOCK + jnp.sum(jnp.where(match, lane, 0.0), axis=1, keepdims=True)
    idx_ref[0] = tok.astype(I32)
    a = aff_ref[...]
    hi = a.astype(BF16)
    r1 = a - hi.astype(F32)
    mid = r1.astype(BF16)
    lo = (r1 - mid.astype(F32)).astype(BF16)
    a_rows = _dot(onehot, hi) + _dot(onehot, mid) + _dot(onehot, lo)
    gate_ref[0] = jnp.sum(jnp.where(match, a_rows, 0.0), axis=1, keepdims=True)


def _route_index(slot, aff2, before, ends, totals, cap):
    rows = slot.shape[0]
    n_e = N_EXPERTS
    nb = rows // n_e
    blocks = pl.BlockSpec((nb, ROUTE_BLOCK), lambda e: (e, 0))
    row = pl.BlockSpec((1, 1, nb), lambda e: (e, 0, 0))
    out = pl.BlockSpec((1, cap, 1), lambda e: (e, 0, 0))
    return pl.pallas_call(
        _route_index_kernel,
        grid=(n_e,),
        in_specs=[blocks, blocks, blocks, row, row],
        out_specs=[out, out],
        out_shape=[jax.ShapeDtypeStruct((n_e, cap, 1), I32), jax.ShapeDtypeStruct((n_e, cap, 1), F32)],
        compiler_params=_params(("parallel",), 48),
        name="route_index",
    )(slot, aff2, before, ends, totals)


def _ffn_kernel(idx_ref, xn_hbm, gate_ref, wg_ref, wu_ref, wd_ref, ye_ref, stage_ref, xe_ref, acc_ref, sem):
    e = pl.program_id(0)
    f = pl.program_id(1)
    n_e = pl.num_programs(0)
    cap = xe_ref.shape[0]
    rows_per_step = cap // (EXPERT_FF // wg_ref.shape[2])
    buf = e % 2

    def row_copy(expert, slot, dst_buf):
        tok = idx_ref[expert * cap + slot]
        return pltpu.make_async_copy(xn_hbm.at[pl.ds(tok, 1), :], stage_ref.at[dst_buf, pl.ds(slot, 1), :],
                                     sem.at[dst_buf])

    def wait_buffer(which):
        pltpu.make_async_copy(xn_hbm.at[pl.ds(0, cap), :], stage_ref.at[which], sem.at[which]).wait()

    @pl.when(f == 0)
    def _():
        @pl.when(e == 0)
        def _():
            def start(pair, c):
                for lane in range(DMA_THREADS):
                    row_copy(0, pair * DMA_THREADS + lane, 0).start(priority=lane)
                return c

            lax.fori_loop(0, cap // DMA_THREADS, start, 0, unroll=4)

        wait_buffer(buf)
        xe_ref[...] = stage_ref[buf].astype(BF16)
        acc_ref[...] = jnp.zeros_like(acc_ref)

    xe = xe_ref[...]
    g = _dot(xe, wg_ref[0].astype(BF16))
    u = _dot(xe, wu_ref[0].astype(BF16))
    h = (_silu(g) * u).astype(BF16)
    acc_ref[...] += _dot(h, wd_ref[0].astype(BF16))

    nxt = jnp.minimum(e + 1, n_e - 1)
    for r in range(rows_per_step):
        row_copy(nxt, f * rows_per_step + r, 1 - buf).start(priority=r % DMA_THREADS)

    @pl.when(f == pl.num_programs(1) - 1)
    def _():
        ye_ref[0] = (acc_ref[...] * gate_ref[0]).astype(BF16)

        @pl.when(e == n_e - 1)
        def _():
            wait_buffer(1 - buf)


def _ffn(idx, xn, gates3, w_gate, w_up, w_down):
    e, cap = idx.shape
    tf = 256
    grid_spec = pltpu.PrefetchScalarGridSpec(
        num_scalar_prefetch=1,
        grid=(e, EXPERT_FF // tf),
        in_specs=[
            pl.BlockSpec(memory_space=pl.ANY),
            pl.BlockSpec((1, cap, 1), lambda ee, f, ix: (ee, 0, 0)),
            pl.BlockSpec((1, D_MODEL, tf), lambda ee, f, ix: (ee, 0, f)),
            pl.BlockSpec((1, D_MODEL, tf), lambda ee, f, ix: (ee, 0, f)),
            pl.BlockSpec((1, tf, D_MODEL), lambda ee, f, ix: (ee, f, 0)),
        ],
        out_specs=pl.BlockSpec((1, cap, D_MODEL), lambda ee, f, ix: (ee, 0, 0)),
        scratch_shapes=[
            pltpu.VMEM((2, cap, D_MODEL), F32),
            pltpu.VMEM((cap, D_MODEL), BF16),
            pltpu.VMEM((cap, D_MODEL), F32),
            pltpu.SemaphoreType.DMA((2,)),
        ],
    )
    return pl.pallas_call(
        _ffn_kernel,
        grid_spec=grid_spec,
        out_shape=jax.ShapeDtypeStruct((e, cap, D_MODEL), BF16),
        compiler_params=_params(("arbitrary", "arbitrary"), 56),
        name="moe_ffn",
    )(idx.reshape(-1), xn, gates3, w_gate, w_up, w_down)


def _combine_kernel(lo_ref, cnt_ref, slot_ref, x1_ref, lo_row_ref, cnt_row_ref, lo_col_ref, cnt_col_ref, ye_hbm,
                    y_ref, stage_ref, sem):
    i = pl.program_id(0)
    nt = pl.num_programs(0)
    n_e = ye_hbm.shape[0]
    tt = slot_ref.shape[0]
    shift = BF16_ROWS.bit_length() - 1

    def plan(tile):
        entries = []
        base = jnp.int32(0)
        for e in range(n_e):
            lo = lo_ref[tile * n_e + e]
            cnt = cnt_ref[tile * n_e + e]
            first = lax.shift_right_logical(lo, shift)
            groups = jnp.where(cnt > 0, lax.shift_right_logical(lo + cnt + (BF16_ROWS - 1), shift) - first, 0)
            entries.append((first * BF16_ROWS, groups, base))
            base = base + groups * BF16_ROWS
        return entries, base

    def for_each_group(tile, buf, fn):
        entries, _ = plan(tile)
        for e, (a0, groups, base) in enumerate(entries):
            def body(j, c, e=e, a0=a0, base=base):
                src = ye_hbm.at[e, pl.ds(pl.multiple_of(a0 + j * BF16_ROWS, BF16_ROWS), BF16_ROWS), :]
                dst = stage_ref.at[buf, pl.ds(pl.multiple_of(base + j * BF16_ROWS, BF16_ROWS), BF16_ROWS), :]
                fn(pltpu.make_async_copy(src, dst, sem.at[buf]), e % DMA_THREADS)
                return c

            lax.fori_loop(0, groups, body, 0)

    buf = i % 2

    @pl.when(i == 0)
    def _():
        stage_ref[...] = jnp.zeros_like(stage_ref)
        for_each_group(0, 0, lambda cp, thread: cp.start(priority=thread))

    @pl.when(i + 1 < nt)
    def _():
        for_each_group(i + 1, 1 - buf, lambda cp, thread: cp.start(priority=thread))

    for_each_group(i, buf, lambda cp, thread: cp.wait())

    def staged(lo, cnt):
        first = lax.shift_right_logical(lo, shift)
        groups = jnp.where(cnt > 0, lax.shift_right_logical(lo + cnt + (BF16_ROWS - 1), shift) - first, 0)
        return first * BF16_ROWS, groups * BF16_ROWS

    a0_row, rows_row = staged(lo_row_ref[0], cnt_row_ref[0])
    _, rows_col = staged(lo_col_ref[0], cnt_col_ref[0])
    ei = lax.broadcasted_iota(I32, (n_e, n_e), 0)
    ej = lax.broadcasted_iota(I32, (n_e, n_e), 1)
    base_row = _dot(rows_row.astype(F32).astype(BF16), (ei < ej).astype(BF16)).astype(I32)
    base_col = _dot((ej < ei).astype(BF16),
                    jnp.broadcast_to(rows_col, (n_e, LANE)).astype(F32).astype(BF16))[:, :1].astype(I32)
    slot = slot_ref[...]
    rel = jnp.where(slot >= 0, slot - a0_row + base_row, -1)
    rel_hi = lax.shift_right_arithmetic(rel, 6).astype(F32).astype(BF16)
    rel_lo = jnp.bitwise_and(rel, 63).astype(F32).astype(BF16)
    _, total = plan(i)
    y_ref[...] = x1_ref[...]

    def k_block(kb, c):
        first = kb * COMBINE_K
        col_e = lax.broadcasted_iota(I32, (n_e, COMBINE_K), 1) + first
        owner = jnp.logical_and(col_e >= base_col, col_e < base_col + rows_col).astype(BF16)
        want = 64.0 * _dot(rel_hi, owner) + _dot(rel_lo, owner)
        col_t = (lax.broadcasted_iota(I32, (tt, COMBINE_K), 1) + first).astype(F32)
        rows = pl.ds(pl.multiple_of(first, COMBINE_K), COMBINE_K)
        y_ref[...] += _dot((want == col_t).astype(BF16), stage_ref[buf, rows, :])
        return c

    lax.fori_loop(0, lax.shift_right_logical(total + (COMBINE_K - 1), COMBINE_K.bit_length() - 1), k_block, 0)


def _combine(lo_flat, cnt_flat, slot, x1, ye):
    t = x1.shape[0]
    n_e = ye.shape[0]
    tt = ROUTE_BLOCK
    stage_rows = n_e * (tt + BF16_ROWS)
    assert stage_rows % COMBINE_K == 0
    nt = t // tt
    lo2 = lo_flat.reshape(nt, n_e)
    cnt2 = cnt_flat.reshape(nt, n_e)
    row_spec = pl.BlockSpec((1, 1, n_e), lambda i, lo, cnt: (i, 0, 0))
    col_spec = pl.BlockSpec((1, n_e, 1), lambda i, lo, cnt: (i, 0, 0))
    grid_spec = pltpu.PrefetchScalarGridSpec(
        num_scalar_prefetch=2,
        grid=(nt,),
        in_specs=[
            pl.BlockSpec((tt, n_e), lambda i, lo, cnt: (i, 0)),
            pl.BlockSpec((tt, D_MODEL), lambda i, lo, cnt: (i, 0)),
            row_spec, row_spec, col_spec, col_spec,
            pl.BlockSpec(memory_space=pl.ANY),
        ],
        out_specs=pl.BlockSpec((tt, D_MODEL), lambda i, lo, cnt: (i, 0)),
        scratch_shapes=[
            pltpu.VMEM((2, stage_rows, D_MODEL), BF16),
            pltpu.SemaphoreType.DMA((2,)),
        ],
    )
    return pl.pallas_call(
        _combine_kernel,
        grid_spec=grid_spec,
        out_shape=jax.ShapeDtypeStruct((t, D_MODEL), F32),
        compiler_params=_params(("arbitrary",), 48),
        name="moe_combine",
    )(lo_flat, cnt_flat, slot, x1, lo2[:, None, :], cnt2[:, None, :], lo2[:, :, None], cnt2[:, :, None], ye)


def _rope_tables(s):
    half = QK_ROPE // 2
    inv_freq = ROPE_THETA ** (-jnp.arange(half, dtype=F32) / half)
    ang = jnp.arange(s, dtype=F32)[:, None] * inv_freq[None, :]
    cos = jnp.cos(ang)
    sin = jnp.sin(ang)
    zeros = jnp.zeros((s, LANE - QK_ROPE), F32)
    return jnp.concatenate([cos, cos, zeros], axis=1), jnp.concatenate([-sin, sin, zeros], axis=1)


def _prepare_weights(norm1_g, w_in, q_a_norm_g, w_q_b, kv_a_norm_g, w_kv_b, q_norm_g, k_norm_g, conv_w,
                     a_log_fwd, a_log_bwd, dt_bias_fwd, dt_bias_bwd, gdn_norm_g, w_o, norm2_g, w_router):
    o_kv = Q_LORA
    o_kr = o_kv + KV_LORA
    o_qkv = o_kr + QK_ROPE
    o_z = o_qkv + QKV_GDN
    o_g = o_z + GDN_WIDTH
    wmisc = jnp.concatenate(
        [w_in[:, o_kr:o_qkv], w_in[:, o_g:], jnp.zeros((D_MODEL, LANE - QK_ROPE - 4 * GDN_HEADS), F32)],
        axis=1).astype(BF16)
    wq = jnp.pad(w_q_b.reshape(Q_LORA, MLA_HEADS, QK_HEAD), ((0, 0), (0, 0), (0, QK_PAD - QK_HEAD)))
    wq = wq.reshape(Q_LORA, MLA_HEADS * QK_PAD).astype(BF16)
    pad_g = lambda g: jnp.pad(g, (0, QK_PAD - QK_HEAD)).reshape(1, QK_PAD)
    alog = jnp.concatenate([a_log_fwd, a_log_bwd]).reshape(2 * GDN_HEADS, 1, 1)
    dtb = jnp.concatenate([dt_bias_fwd, dt_bias_bwd]).reshape(2 * GDN_HEADS, 1, 1)
    return dict(
        g1=norm1_g.reshape(1, D_MODEL), wz=w_in[:, o_z:o_g].astype(BF16), wlat=w_in[:, :o_kr].astype(BF16),
        wqkv=w_in[:, o_qkv:o_z].astype(BF16), wmisc=wmisc,
        gqa=q_a_norm_g.reshape(1, Q_LORA), gkva=kv_a_norm_g.reshape(1, KV_LORA),
        wq=wq, wkv=w_kv_b.astype(BF16), gq=pad_g(q_norm_g), gk=pad_g(k_norm_g),
        conv_w=conv_w,
        alog=jnp.broadcast_to(alog, (2 * GDN_HEADS, 1, LANE)), dtb=jnp.broadcast_to(dtb, (2 * GDN_HEADS, 1, LANE)),
        gn=gdn_norm_g.reshape(1, GDN_DV), wo=w_o.astype(BF16), g2=norm2_g.reshape(1, D_MODEL), wr=w_router,
    )


def _route(aff, cap):
    t, n_e = aff.shape
    nb = t // ROUTE_BLOCK
    aff3 = aff.T.reshape(n_e, nb, ROUTE_BLOCK)
    slot, before, total = _route_select(aff3, cap)
    before_b = before[:, 0].reshape(n_e, nb)
    total_b = total[:, 0].reshape(n_e, nb)
    idx, gates = _route_index(slot, aff3.reshape(n_e * nb, ROUTE_BLOCK), before,
                              (before_b + total_b).reshape(n_e, 1, nb), total_b.reshape(n_e, 1, nb), cap)
    slot_by_token = slot.reshape(n_e, t).T
    return idx.reshape(n_e, cap), gates, slot_by_token, before_b.T.reshape(-1), total_b.T.reshape(-1)


def _encoder_layer(x, p, w_gate, w_up, w_down):
    bt, s, _ = x.shape
    t = bt * s
    x2d = x.reshape(t, D_MODEL)
    oa, ob = _in_proj(x2d, p["g1"], p["wz"], p["wlat"], p["wqkv"], p["wmisc"])

    cos, sin = _rope_tables(s)
    q, k, v = _mla_prep(oa, ob, cos, sin, p["gqa"], p["gkva"], p["wq"], p["wkv"], p["gq"], p["gk"], bt, s)
    mla_out = _attention(q, k, v)

    ob3 = ob.reshape(bt, s, NB)
    qkv_h = _conv(ob3, p["conv_w"])
    g_raw = ob3[:, :, QKV_GDN + QK_ROPE:QKV_GDN + QK_ROPE + 4 * GDN_HEADS]
    g_t = jnp.swapaxes(g_raw, 1, 2).reshape(bt, 4 * GDN_HEADS, s // LANE, LANE)
    g_out = _gates(g_t, p["alog"], p["dtb"]).reshape(bt, 4, GDN_HEADS, s)
    g_rows = g_out[:, :2].reshape(bt, 2 * GDN_HEADS, s // CHUNK, CHUNK)
    g_cols = jnp.transpose(g_out, (0, 2, 3, 1))
    o_fwd, o_bwd = _gdn_scan(_gdn_prep(qkv_h, g_rows, g_cols), g_rows)

    x1, xn, aff = _out_proj(x2d, mla_out.reshape(t, MLA_WIDTH), o_fwd.reshape(t, GDN_WIDTH),
                            o_bwd.reshape(t, GDN_WIDTH), oa, p["gn"], p["wo"], p["g2"], p["wr"])

    cap = max(1, CAPACITY_FACTOR * t // N_EXPERTS)
    idx, gates, slot, lo, cnt = _route(aff, cap)
    ye = _ffn(idx, xn, gates, w_gate, w_up, w_down)
    y = _combine(lo, cnt, slot, x1, ye)
    return y.reshape(bt, s, D_MODEL)


def kernel(x_prompt, x_sample, norm1_g, w_in, q_a_norm_g, w_q_b, kv_a_norm_g, w_kv_b, q_norm_g, k_norm_g, conv_w,
           a_log_fwd, a_log_bwd, dt_bias_fwd, dt_bias_bwd, gdn_norm_g, w_o, norm2_g, w_router, w_gate, w_up, w_down):
    y_prompt = x_prompt
    y_sample = x_sample
    for l in range(norm1_g.shape[0]):
        p = _prepare_weights(norm1_g[l], w_in[l], q_a_norm_g[l], w_q_b[l], kv_a_norm_g[l], w_kv_b[l], q_norm_g[l],
                             k_norm_g[l], conv_w[l], a_log_fwd[l], a_log_bwd[l], dt_bias_fwd[l], dt_bias_bwd[l],
                             gdn_norm_g[l], w_o[l], norm2_g[l], w_router[l])
        y_prompt = _encoder_layer(y_prompt, p, w_gate[l], w_up[l], w_down[l])
        y_sample = _encoder_layer(y_sample, p, w_gate[l], w_up[l], w_down[l])
    return (y_prompt, y_sample)
```
